```python
import jax
import jax.numpy as jnp
from jax import lax
import numpy as np

D_MODEL = 4096
BATCH = 4
SEQ = 4096
DEPTH = 2
DEC_BATCH = 2
DEC_SEQ = 4096
PAST_LEN = 128

N_MEM = 256
D_MIX = D_MODEL
RMS_EPS = 1e-6
D_A = D_MIX // 4
A_HEAD_DIM = 128
A_EXPAND = 128
A_HEADS = D_A // A_HEAD_DIM
D_AK = A_HEADS * A_EXPAND
CHUNK = 64
D_B = D_MIX // 2
B_V = 128
B_HEADS = D_B // B_V
B_NOPE = 128
B_ROPE = 64
Q_LORA = D_MODEL // 4
KV_LORA = D_MODEL // 8
ROPE_THETA = 10000.0
Q_BLOCK = 128
D_C = D_MIX - D_A - D_B
C_GROUPS = 16
C_GROUP_DIM = D_C // C_GROUPS
CONV_W = 3
IN_SPLITS = (D_AK, D_AK, D_AK, D_A, D_A, Q_LORA, KV_LORA, B_ROPE, D_C, D_C, D_C)
D_IN = 3 * D_AK + 2 * D_A + Q_LORA + KV_LORA + B_ROPE + 3 * D_C
X_HEADS = 4
X_HEAD_DIM = D_MODEL // 16
X_W = X_HEADS * X_HEAD_DIM
N_GROUPS = 8
EXP_PER_GROUP = 8
N_EXPERTS = N_GROUPS * EXP_PER_GROUP
TOP_K = 2
D_EXPERT = D_MODEL // 4
EXPERT_BLOCK = 128

kernel_name = "hybrid_bidir_hgrn2_mla_conv_hmoe_encoder"


def _rmsnorm(x, g):
    xf = x.astype(jnp.float32)
    y = xf * lax.rsqrt(jnp.mean(xf * xf, axis=-1, keepdims=True) + RMS_EPS)
    return (y * g.astype(jnp.float32)).astype(x.dtype)


def _split_cols(z):
    out, s = [], 0
    for w in IN_SPLITS:
        out.append(z[..., s:s + w])
        s += w
    return out


def _gla_chunk_scan(q, k, v, logf):
    bsz, L, H, _ = q.shape
    dv = v.shape[-1]
    n = L // CHUNK

    def to_chunks(t):
        return t.reshape(bsz, n, CHUNK, H, t.shape[-1]).transpose(1, 0, 3, 2, 4)

    qc, kc, vc, gc = to_chunks(q), to_chunks(k), to_chunks(v), to_chunks(logf)
    lower = jnp.tril(jnp.ones((CHUNK, CHUNK), bool))[:, :, None]

    def step(S, inp):
        qi, ki, vi, gi = inp
        b = jnp.cumsum(gi, axis=-2)
        o_inter = jnp.einsum('bhck,bhkv->bhcv', qi * jnp.exp(b), S)
        diff = b[:, :, :, None, :] - b[:, :, None, :, :]
        decay = jnp.exp(jnp.where(lower, diff, -jnp.inf))
        att = jnp.einsum('bhtk,bhsk,bhtsk->bhts', qi, ki, decay)
        o_intra = jnp.einsum('bhts,bhsv->bhtv', att, vi)
        b_last = b[:, :, -1:, :]
        S_new = jnp.exp(b_last[:, :, 0, :])[..., None] * S + jnp.einsum('bhsk,bhsv->bhkv', ki * jnp.exp(b_last - b), vi)
        return S_new, o_inter + o_intra

    S0 = jnp.zeros((bsz, H, q.shape[-1], dv), q.dtype)
    _, o = lax.scan(step, S0, (qc, kc, vc, gc))
    return o.transpose(1, 0, 3, 2, 4).reshape(bsz, L, H, dv)


def _hgrn2(a_q, a_ff, a_fb, a_i, a_g, lb_fwd, lb_bwd, norm_g):
    bsz, L, _ = a_q.shape

    def heads(t, d):
        return t.reshape(bsz, L, A_HEADS, d).astype(jnp.float32)

    qh = jax.nn.silu(heads(a_q, A_EXPAND))
    vh = heads(a_i, A_HEAD_DIM)

    def gates(f, lb):
        fh = heads(f, A_EXPAND)
        lb = lb.reshape(A_HEADS, A_EXPAND)
        logf = jnp.logaddexp(jnp.log(lb), jnp.log1p(-lb) + jax.nn.log_sigmoid(fh))
        k = (1.0 - lb) * jax.nn.sigmoid(-fh)
        return k, logf

    k_f, g_f = gates(a_ff, lb_fwd)
    k_b, g_b = gates(a_fb, lb_bwd)
    flip = lambda t: jnp.flip(t, axis=1)
    o = _gla_chunk_scan(qh, k_f, vh, g_f) + flip(_gla_chunk_scan(flip(qh), flip(k_b), flip(vh), flip(g_b)))
    o = _rmsnorm(o, norm_g) * jax.nn.silu(heads(a_g, A_HEAD_DIM))
    return o.reshape(bsz, L, D_A).astype(a_q.dtype)


def _rope_tables(L):
    inv = 1.0 / (ROPE_THETA ** (jnp.arange(0, B_ROPE, 2, dtype=jnp.float32) / B_ROPE))
    ang = jnp.arange(L, dtype=jnp.float32)[:, None] * inv[None, :]
    return jnp.cos(ang), jnp.sin(ang)


def _apply_rope(x, cos, sin):
    half = x.shape[-1] // 2
    x1, x2 = x[..., :half].astype(jnp.float32), x[..., half:].astype(jnp.float32)
    return jnp.concatenate([x1 * cos - x2 * sin, x2 * cos + x1 * sin], axis=-1).astype(x.dtype)


def _mla(b_cq, b_ckv, b_kr, g_cq, w_uq, g_ckv, w_ukv, norm_g):
    bsz, L, _ = b_cq.shape
    cos, sin = _rope_tables(L)
    q = (_rmsnorm(b_cq, g_cq) @ w_uq).reshape(bsz, L, B_HEADS, B_NOPE + B_ROPE)
    q_nope = q[..., :B_NOPE]
    q_rope = _apply_rope(q[..., B_NOPE:], cos[:, None, :], sin[:, None, :])
    kv = (_rmsnorm(b_ckv, g_ckv) @ w_ukv).reshape(bsz, L, B_HEADS, B_NOPE + B_V)
    k_nope, v = kv[..., :B_NOPE], kv[..., B_NOPE:]
    k_rope = _apply_rope(b_kr, cos, sin)
    scale = (B_NOPE + B_ROPE) ** -0.5
    nb = L // Q_BLOCK

    def attend(blk):
        qn, qr = blk
        s = (jnp.einsum('bqhd,bkhd->bhqk', qn, k_nope) + jnp.einsum('bqhr,bkr->bhqk', qr, k_rope)).astype(jnp.float32) * scale
        p = jax.nn.softmax(s, axis=-1).astype(v.dtype)
        return jnp.einsum('bhqk,bkhv->bqhv', p, v)

    def blocks(t):
        return jnp.moveaxis(t.reshape(bsz, nb, Q_BLOCK, *t.shape[2:]), 1, 0)

    o = lax.map(attend, (blocks(q_nope), blocks(q_rope)))
    o = jnp.moveaxis(o, 0, 1).reshape(bsz, L, B_HEADS, B_V)
    return _rmsnorm(o, norm_g).reshape(bsz, L, D_B)


def _short_conv(c_b, c_c, c_h, conv_w, conv_b, norm_g):
    bsz, L, _ = c_h.shape
    u = c_c * c_h
    up = jnp.pad(u, ((0, 0), (1, 1), (0, 0)))
    y = up[:, :L] * conv_w[0] + up[:, 1:L + 1] * conv_w[1] + up[:, 2:] * conv_w[2] + conv_b
    o = (c_b * y).reshape(bsz, L, C_GROUPS, C_GROUP_DIM)
    return _rmsnorm(o, norm_g).reshape(bsz, L, D_C)


def _cross_attn(h, m, w_q, w_k, w_v, w_o):
    bsz, L, _ = h.shape
    nm = m.shape[1]
    q = (h @ w_q).reshape(bsz, L, X_HEADS, X_HEAD_DIM)
    k = (m @ w_k).reshape(bsz, nm, X_HEADS, X_HEAD_DIM)
    v = (m @ w_v).reshape(bsz, nm, X_HEADS, X_HEAD_DIM)
    s = jnp.einsum('bqhd,bmhd->bhqm', q, k).astype(jnp.float32) * (X_HEAD_DIM ** -0.5)
    p = jax.nn.softmax(s, axis=-1).astype(v.dtype)
    o = jnp.einsum('bhqm,bmhd->bqhd', p, v).reshape(bsz, L, X_W)
    return o @ w_o


def _hier_moe(x, w_rg, b_rg, w_re, b_re, w_gate, w_up, w_down):
    bsz, L, D = x.shape
    T = bsz * L
    xt = x.reshape(T, D)
    g_prob = jax.nn.softmax((xt @ w_rg + b_rg).astype(jnp.float32), axis=-1)
    g_val, g_idx = lax.top_k(g_prob, 1)
    e_logits = (xt @ w_re + b_re).astype(jnp.float32).reshape(T, N_GROUPS, EXP_PER_GROUP)
    e_in = jnp.take_along_axis(e_logits, g_idx[:, :, None], axis=1)[:, 0]
    e_val, e_idx = lax.top_k(e_in, TOP_K)
    e_w = jax.nn.softmax(e_val, axis=-1) * g_val
    expert = g_idx * EXP_PER_GROUP + e_idx
    flat_e = expert.reshape(-1)
    flat_tok = jnp.repeat(jnp.arange(T), TOP_K)
    flat_w = e_w.reshape(-1)
    order = jnp.argsort(flat_e)
    se, stok, sw = flat_e[order], flat_tok[order], flat_w[order]
    counts = jnp.bincount(flat_e, length=N_EXPERTS)
    start = jnp.cumsum(counts) - counts
    pcounts = (counts + EXPERT_BLOCK - 1) // EXPERT_BLOCK * EXPERT_BLOCK
    pend = jnp.cumsum(pcounts)
    pstart = pend - pcounts
    dest = pstart[se] + jnp.arange(T * TOP_K) - start[se]
    n_rows = T * TOP_K + N_EXPERTS * EXPERT_BLOCK
    n_blocks = n_rows // EXPERT_BLOCK
    xs = jnp.zeros((n_rows, D), x.dtype).at[dest].set(xt[stok])
    blk_exp = jnp.minimum(jnp.searchsorted(pend, jnp.arange(n_blocks) * EXPERT_BLOCK, side='right'), N_EXPERTS - 1)

    def run_block(args):
        xb, e = args
        hb = jax.nn.silu(xb @ w_gate[e]) * (xb @ w_up[e])
        return hb @ w_down[e]

    ys = lax.map(run_block, (xs.reshape(n_blocks, EXPERT_BLOCK, D), blk_exp)).reshape(n_rows, D)
    out = jax.ops.segment_sum(ys[dest] * sw[:, None].astype(ys.dtype), stok, num_segments=T)
    return out.reshape(bsz, L, D).astype(x.dtype)


def _encode(x, mem, g_mix, w_in, a_lb_fwd, a_lb_bwd, a_norm, b_g_cq, b_w_uq, b_g_ckv, b_w_ukv, b_norm,
            c_conv_w, c_conv_b, c_norm, w_out, g_xattn, g_mem, w_xq, w_xk, w_xv, w_xo,
            g_ffn, w_rg, b_rg, w_re, b_re, w_gate, w_up, w_down, g_final):
    lbf = jnp.cumsum(jax.nn.softmax(a_lb_fwd.astype(jnp.float32), axis=0), axis=0)
    lbf = lbf - lbf[0]
    lbb = jnp.cumsum(jax.nn.softmax(a_lb_bwd.astype(jnp.float32), axis=0), axis=0)
    lbb = lbb - lbb[0]
    for l in range(DEPTH):
        h = _rmsnorm(x, g_mix[l])
        (a_q, a_ff, a_fb, a_i, a_g, b_cq, b_ckv, b_kr, c_b, c_c, c_h) = _split_cols(h @ w_in[l])
        o_a = _hgrn2(a_q, a_ff, a_fb, a_i, a_g, lbf[l], lbb[l], a_norm[l])
        o_b = _mla(b_cq, b_ckv, b_kr, b_g_cq[l], b_w_uq[l], b_g_ckv[l], b_w_ukv[l], b_norm[l])
        o_c = _short_conv(c_b, c_c, c_h, c_conv_w[l], c_conv_b[l], c_norm[l])
        x = x + jnp.concatenate([o_a, o_b, o_c], axis=-1) @ w_out[l]
        x = x + _cross_attn(_rmsnorm(x, g_xattn[l]), _rmsnorm(mem, g_mem[l]), w_xq[l], w_xk[l], w_xv[l], w_xo[l])
        x = x + _hier_moe(_rmsnorm(x, g_ffn[l]), w_rg[l], b_rg[l], w_re[l], b_re[l], w_gate[l], w_up[l], w_down[l])
    return _rmsnorm(x, g_final)


def setup_inputs(seed: int = 0) -> dict:
    key = jax.random.key(seed)
    ks = iter(jax.random.split(key, 48))

    def nrm(shape, scale):
        return jax.random.normal(next(ks), shape, jnp.float32) * scale

    def gain(shape):
        return 1.0 + nrm(shape, 0.02)

    return {
        'x_prompt': nrm((BATCH, SEQ, D_MODEL), 1.0),
        'x_sample': nrm((DEC_BATCH, DEC_SEQ, D_MODEL), 1.0),
        'mem_prompt': nrm((BATCH, N_MEM, D_MODEL), 1.0),
        'mem_sample': nrm((DEC_BATCH, N_MEM, D_MODEL), 1.0),
        'g_mix': gain((DEPTH, D_MODEL)),
        'w_in': nrm((DEPTH, D_MODEL, D_IN), D_MODEL ** -0.5),
        'a_lb_fwd': nrm((DEPTH, D_AK), 0.5),
        'a_lb_bwd': nrm((DEPTH, D_AK), 0.5),
        'a_norm': gain((DEPTH, A_HEADS, A_HEAD_DIM)),
        'b_g_cq': gain((DEPTH, Q_LORA)),
        'b_w_uq': nrm((DEPTH, Q_LORA, B_HEADS * (B_NOPE + B_ROPE)), Q_LORA ** -0.5),
        'b_g_ckv': gain((DEPTH, KV_LORA)),
        'b_w_ukv': nrm((DEPTH, KV_LORA, B_HEADS * (B_NOPE + B_V)), KV_LORA ** -0.5),
        'b_norm': gain((DEPTH, B_HEADS, B_V)),
        'c_conv_w': nrm((DEPTH, CONV_W, D_C), CONV_W ** -0.5),
        'c_conv_b': nrm((DEPTH, D_C), 0.01),
        'c_norm': gain((DEPTH, C_GROUPS, C_GROUP_DIM)),
        'w_out': nrm((DEPTH, D_MIX, D_MODEL), D_MIX ** -0.5),
        'g_xattn': gain((DEPTH, D_MODEL)),
        'g_mem': gain((DEPTH, D_MODEL)),
        'w_xq': nrm((DEPTH, D_MODEL, X_W), D_MODEL ** -0.5),
        'w_xk': nrm((DEPTH, D_MODEL, X_W), D_MODEL ** -0.5),
        'w_xv': nrm((DEPTH, D_MODEL, X_W), D_MODEL ** -0.5),
        'w_xo': nrm((DEPTH, X_W, D_MODEL), X_W ** -0.5),
        'g_ffn': gain((DEPTH, D_MODEL)),
        'w_rg': nrm((DEPTH, D_MODEL, N_GROUPS), D_MODEL ** -0.5),
        'b_rg': nrm((DEPTH, N_GROUPS), 0.01),
        'w_re': nrm((DEPTH, D_MODEL, N_EXPERTS), D_MODEL ** -0.5),
        'b_re': nrm((DEPTH, N_EXPERTS), 0.01),
        'w_gate': nrm((DEPTH, N_EXPERTS, D_MODEL, D_EXPERT), D_MODEL ** -0.5),
        'w_up': nrm((DEPTH, N_EXPERTS, D_MODEL, D_EXPERT), D_MODEL ** -0.5),
        'w_down': nrm((DEPTH, N_EXPERTS, D_EXPERT, D_MODEL), D_EXPERT ** -0.5),
        'g_final': gain((D_MODEL,)),
    }


def reference(x_prompt, x_sample, mem_prompt, mem_sample, g_mix, w_in, a_lb_fwd, a_lb_bwd, a_norm,
              b_g_cq, b_w_uq, b_g_ckv, b_w_ukv, b_norm, c_conv_w, c_conv_b, c_norm, w_out,
              g_xattn, g_mem, w_xq, w_xk, w_xv, w_xo, g_ffn, w_rg, b_rg, w_re, b_re,
              w_gate, w_up, w_down, g_final):
    weights = (g_mix, w_in, a_lb_fwd, a_lb_bwd, a_norm, b_g_cq, b_w_uq, b_g_ckv, b_w_ukv, b_norm,
               c_conv_w, c_conv_b, c_norm, w_out, g_xattn, g_mem, w_xq, w_xk, w_xv, w_xo,
               g_ffn, w_rg, b_rg, w_re, b_re, w_gate, w_up, w_down, g_final)
    y_prompt = _encode(x_prompt, mem_prompt, *weights)
    y_sample = _encode(x_sample, mem_sample, *weights)
    return (y_prompt, y_sample)
```

```python
import functools

import jax
import jax.numpy as jnp
from jax import lax
from jax.experimental import pallas as pl
from jax.experimental.pallas import tpu as pltpu

F32 = jnp.float32
BF16 = jnp.bfloat16

RMS_EPS = 1e-6
LANES = 128
SUBLANES = 8
MXU_DIM = 256
VMEM_LIMIT = 56 * 1024 * 1024

A_HEAD_DIM = 128
CHUNK = 64
B_V = 128
B_NOPE = 128
B_ROPE = 64
ROPE_THETA = 10000.0
C_GROUPS = 16
X_HEADS = 4
N_GROUPS = 8
EXP_PER_GROUP = 8
N_EXPERTS = N_GROUPS * EXP_PER_GROUP
TOP_K = 2
ROUTER_LANES = 128

NT_DIMS = (((1,), (1,)), ((), ()))
TN_DIMS = (((0,), (0,)), ((), ()))


def _tile(n, pref, align=LANES):
    if n <= pref:
        return n
    t = (pref // align) * align
    while t >= align:
        if n % t == 0:
            return t
        t -= align
    return n


def _round_up(n, m):
    return (n + m - 1) // m * m


def _params(*sem):
    return pltpu.CompilerParams(dimension_semantics=sem, vmem_limit_bytes=VMEM_LIMIT)


def _sigmoid(x):
    return 1.0 / (1.0 + jnp.exp(-x))


def _rms(x, gain):
    ms = jnp.mean(x * x, axis=-1, keepdims=True)
    return x * lax.rsqrt(ms + RMS_EPS) * gain


def _mm_kernel(*refs, n_lhs, normed, has_res):
    refs = list(refs)
    gain_ref = refs.pop(0) if normed else None
    lhs = refs[:n_lhs]
    ws = refs[n_lhs:2 * n_lhs]
    rest = refs[2 * n_lhs:]
    res_ref = rest.pop(0) if has_res else None
    out_ref = rest.pop(0)
    if normed:
        xn_ref = rest.pop(0)

        @pl.when(pl.program_id(1) == 0)
        def _():
            xn_ref[...] = _rms(lhs[0][...].astype(F32), gain_ref[...]).astype(BF16)

        acc = jnp.dot(xn_ref[...], ws[0][...], preferred_element_type=F32)
    else:
        acc = None
        for l_ref, w_ref in zip(lhs, ws):
            d = jnp.dot(l_ref[...].astype(BF16), w_ref[...], preferred_element_type=F32)
            acc = d if acc is None else acc + d
    if has_res:
        acc = acc + res_ref[...]
    out_ref[...] = acc.astype(out_ref.dtype)


def _matmul(lhs_list, w, *, gain=None, res=None, out_dtype=F32, tm=512, tn=512, name="mm"):
    m = lhs_list[0][0].shape[0]
    n = w.shape[1]
    tm = _tile(m, tm, SUBLANES)
    tn = _tile(n, tn)
    normed = gain is not None
    assert not normed or len(lhs_list) == 1
    in_specs, args = [], []
    if normed:
        k0 = lhs_list[0][1]
        in_specs.append(pl.BlockSpec((1, k0), lambda i, j: (0, 0)))
        args.append(gain.reshape(1, k0).astype(F32))
    for arr, k, cb, _ in lhs_list:
        in_specs.append(pl.BlockSpec((tm, k), lambda i, j, cb=cb: (i, cb)))
        args.append(arr)
    for _, k, _, rb in lhs_list:
        in_specs.append(pl.BlockSpec((k, tn), lambda i, j, rb=rb: (rb, j)))
        args.append(w)
    if res is not None:
        in_specs.append(pl.BlockSpec((tm, tn), lambda i, j: (i, j)))
        args.append(res)
    scratch = [pltpu.VMEM((tm, lhs_list[0][1]), BF16)] if normed else []
    return pl.pallas_call(
        functools.partial(_mm_kernel, n_lhs=len(lhs_list), normed=normed, has_res=res is not None),
        grid=(m // tm, n // tn),
        in_specs=in_specs,
        out_specs=pl.BlockSpec((tm, tn), lambda i, j: (i, j)),
        out_shape=jax.ShapeDtypeStruct((m, n), out_dtype),
        scratch_shapes=scratch,
        compiler_params=_params("parallel", "arbitrary"),
        name=name,
    )(*args)


def _gla_chunk(dirn, r0, q_ref, f_ref, v_ref, lbp_ref, st_ref, o_ref, bsc, ksc, msc, rsc, tri, ones):
    c = CHUNK
    fwd = dirn == 0
    rows = pl.ds(r0, c)
    x = f_ref[rows, :]
    e = jnp.exp(-jnp.abs(x))
    inv = 1.0 / (1.0 + e)
    log_sig = jnp.minimum(x, 0.0) - jnp.log1p(e)
    sig_neg = jnp.where(x >= 0, e * inv, inv)
    la = lbp_ref[3 * dirn:3 * dirn + 1, :]
    lc = lbp_ref[3 * dirn + 1:3 * dirn + 2, :] + log_sig
    logf = jnp.maximum(la, lc) + jnp.log1p(jnp.exp(-jnp.abs(la - lc)))
    kk = lbp_ref[3 * dirn + 2:3 * dirn + 3, :] * sig_neg

    hi = logf.astype(BF16)
    r1 = logf - hi.astype(F32)
    mid = r1.astype(BF16)
    lo = (r1 - mid.astype(F32)).astype(BF16)
    b = (jnp.dot(tri, hi, preferred_element_type=F32) + jnp.dot(tri, mid, preferred_element_type=F32)
         + jnp.dot(tri, lo, preferred_element_type=F32))

    qx = q_ref[rows, :]
    qs = qx * _sigmoid(qx)
    vv = v_ref[rows, :]
    vb = vv.astype(BF16)
    b_end = b[c - 1:c, :] if fwd else b[0:1, :]
    st = st_ref[dirn]
    o = lax.dot_general((qs * jnp.exp(b)).astype(BF16), st.astype(BF16), NT_DIMS, preferred_element_type=F32)

    bsc[dirn] = b
    ksc[dirn] = kk
    nblk = c // SUBLANES
    pieces = []
    off = 0
    for j in range(nblk):
        t0, t1 = (SUBLANES * j, c) if fwd else (0, SUBLANES * (j + 1))
        qb = qs[t0:t1]
        bb = b[t0:t1]
        for s in range(SUBLANES * j, SUBLANES * (j + 1)):
            bs = bsc[dirn, pl.ds(s, 1), :]
            ks = ksc[dirn, pl.ds(s, 1), :]
            msc[dirn, off:off + (t1 - t0), :] = (qb * ks) * jnp.exp(jnp.minimum(bb - bs, 0.0))
            pieces.append((s, t0, t1, off))
            off += t1 - t0
    rsc[dirn] = jnp.dot(msc[dirn].astype(BF16), ones, preferred_element_type=F32)
    lane = lax.broadcasted_iota(jnp.int32, (SUBLANES, LANES), 1)
    sub = lax.broadcasted_iota(jnp.int32, (SUBLANES, LANES), 0)
    att = [jnp.zeros((SUBLANES, LANES), F32) for _ in range(nblk)]
    for s, t0, t1, off in pieces:
        for i in range(t0 // SUBLANES, t1 // SUBLANES):
            o8 = off + SUBLANES * (i - t0 // SUBLANES)
            att[i] = jnp.where(lane == s, rsc[dirn, o8:o8 + SUBLANES, :], att[i])
    for i in range(nblk):
        t = sub + SUBLANES * i
        att[i] = jnp.where((lane <= t) if fwd else (lane >= t), att[i], 0.0)
    att = jnp.concatenate(att, axis=0)[:, :c]
    o = o + jnp.dot(att.astype(BF16), vb, preferred_element_type=F32)

    kt = kk * jnp.exp(b_end - b)
    st_ref[dirn] = st * jnp.exp(b_end) + lax.dot_general(vb, kt.astype(BF16), TN_DIMS, preferred_element_type=F32)
    o_ref[rows, :] = o


def _hgrn_kernel(q_ref, ff_ref, fb_ref, v_ref, g_ref, lbp_ref, gn_ref, out_ref,
                 of_ref, ob_ref, st_ref, bsc, ksc, msc, rsc, *, seq):
    c = CHUNK
    n = seq // c
    st_ref[...] = jnp.zeros_like(st_ref)
    row = lax.broadcasted_iota(jnp.int32, (c, c), 0)
    col = lax.broadcasted_iota(jnp.int32, (c, c), 1)
    tri_f = jnp.where(col <= row, 1.0, 0.0).astype(BF16)
    tri_b = jnp.where(col >= row, 1.0, 0.0).astype(BF16)
    ones = jnp.ones((LANES, LANES), BF16)

    def body(i, carry):
        _gla_chunk(0, pl.multiple_of(i * c, c), q_ref, ff_ref, v_ref, lbp_ref, st_ref, of_ref,
                   bsc, ksc, msc, rsc, tri_f, ones)
        _gla_chunk(1, pl.multiple_of((n - 1 - i) * c, c), q_ref, fb_ref, v_ref, lbp_ref, st_ref, ob_ref,
                   bsc, ksc, msc, rsc, tri_b, ones)
        return carry

    lax.fori_loop(0, n, body, 0)

    rt = _tile(seq, 256, SUBLANES)

    def fin(i, carry):
        rows = pl.ds(pl.multiple_of(i * rt, rt), rt)
        o = of_ref[rows, :] + ob_ref[rows, :]
        g = g_ref[rows, :]
        out_ref[rows, :] = (_rms(o, gn_ref[...]) * (g * _sigmoid(g))).astype(out_ref.dtype)
        return carry

    lax.fori_loop(0, seq // rt, fin, 0)


def _hgrn(z, lbp, gn, *, nseq, seq, heads, cb_q, cb_ff, cb_fb, cb_v, cb_g):
    def zspec(cb):
        return pl.BlockSpec((seq, LANES), lambda b, h, cb=cb: (b, cb + h))

    n_intra = SUBLANES * sum(range(SUBLANES, CHUNK + 1, SUBLANES))
    return pl.pallas_call(
        functools.partial(_hgrn_kernel, seq=seq),
        grid=(nseq, heads),
        in_specs=[zspec(cb_q), zspec(cb_ff), zspec(cb_fb), zspec(cb_v), zspec(cb_g),
                  pl.BlockSpec((6, LANES), lambda b, h: (0, h)),
                  pl.BlockSpec((1, LANES), lambda b, h: (0, h))],
        out_specs=pl.BlockSpec((seq, LANES), lambda b, h: (b, h)),
        out_shape=jax.ShapeDtypeStruct((nseq * seq, heads * LANES), BF16),
        scratch_shapes=[pltpu.VMEM((seq, LANES), F32), pltpu.VMEM((seq, LANES), F32),
                        pltpu.VMEM((2, LANES, LANES), F32),
                        pltpu.VMEM((2, CHUNK, LANES), F32), pltpu.VMEM((2, CHUNK, LANES), F32),
                        pltpu.VMEM((2, n_intra, LANES), F32), pltpu.VMEM((2, n_intra, LANES), F32)],
        compiler_params=_params("parallel", "parallel"),
        name="hgrn",
    )(z, z, z, z, z, lbp, gn)


def _rope128(x, ct, st):
    rot = pltpu.roll(x, 32, 1) - pltpu.roll(x, 96, 1)
    return x * ct + rot * st


def _mla_kernel(q_ref, kn_ref, v_ref, kr_ref, ctq_ref, stq_ref, ctk_ref, stk_ref, gn_ref, out_ref,
                kcat_ref, *, seq, tk, scale):
    qi = pl.program_id(2)
    rt = _tile(seq, 512, SUBLANES)

    @pl.when(qi == 0)
    def _():
        def build(i, carry):
            rows = pl.ds(pl.multiple_of(i * rt, rt), rt)
            krr = _rope128(kr_ref[rows, :], ctk_ref[rows, :], stk_ref[rows, :])
            kcat_ref[rows, 0:LANES] = kn_ref[rows, :]
            kcat_ref[rows, LANES:2 * LANES] = krr.astype(BF16)
            return carry

        lax.fori_loop(0, seq // rt, build, 0)

    q = q_ref[...].astype(F32)
    qrr = _rope128(q[:, LANES:], ctq_ref[...], stq_ref[...])
    qc = (jnp.concatenate([q[:, :LANES], qrr], axis=1) * scale).astype(BF16)
    tq = qc.shape[0]

    def body(kc, carry):
        m, l, acc = carry
        rows = pl.ds(pl.multiple_of(kc * tk, tk), tk)
        s = lax.dot_general(qc, kcat_ref[rows, :], NT_DIMS, preferred_element_type=F32)
        m_new = jnp.maximum(m, jnp.max(s, axis=-1, keepdims=True))
        alpha = jnp.exp(m - m_new)
        p = jnp.exp(s - m_new)
        l = alpha * l + jnp.sum(p, axis=-1, keepdims=True)
        acc = alpha * acc + jnp.dot(p.astype(BF16), v_ref[rows, :], preferred_element_type=F32)
        return m_new, l, acc

    m0 = jnp.full((tq, 1), -jnp.inf, F32)
    l0 = jnp.zeros((tq, 1), F32)
    a0 = jnp.zeros((tq, B_V), F32)
    _, l, acc = lax.fori_loop(0, seq // tk, body, (m0, l0, a0))
    o = acc / l
    out_ref[...] = _rms(o, gn_ref[...]).astype(out_ref.dtype)


def _mla(q_raw, kv_raw, z, ct, st, gn, *, nseq, seq, heads, cb_kr, tq=256, tk=512):
    tq = _tile(seq, tq, SUBLANES)
    tk = _tile(seq, tk, SUBLANES)
    nq = seq // tq
    scale = float((B_NOPE + B_ROPE) ** -0.5)
    return pl.pallas_call(
        functools.partial(_mla_kernel, seq=seq, tk=tk, scale=scale),
        grid=(nseq, heads, nq),
        in_specs=[pl.BlockSpec((tq, 2 * LANES), lambda b, h, i: (b * nq + i, h)),
                  pl.BlockSpec((seq, LANES), lambda b, h, i: (b, h)),
                  pl.BlockSpec((seq, LANES), lambda b, h, i: (b, heads + h)),
                  pl.BlockSpec((seq, LANES), lambda b, h, i: (b, cb_kr)),
                  pl.BlockSpec((tq, LANES), lambda b, h, i: (i, 0)),
                  pl.BlockSpec((tq, LANES), lambda b, h, i: (i, 0)),
                  pl.BlockSpec((seq, LANES), lambda b, h, i: (0, 0)),
                  pl.BlockSpec((seq, LANES), lambda b, h, i: (0, 0)),
                  pl.BlockSpec((1, LANES), lambda b, h, i: (0, h))],
        out_specs=pl.BlockSpec((tq, LANES), lambda b, h, i: (b * nq + i, h)),
        out_shape=jax.ShapeDtypeStruct((nseq * seq, heads * B_V), BF16),
        scratch_shapes=[pltpu.VMEM((seq, 2 * LANES), BF16)],
        compiler_params=_params("parallel", "parallel", "arbitrary"),
        name="mla_attn",
    )(q_raw, kv_raw, kv_raw, z, ct, st, ct, st, gn)


def _conv_kernel(cb_ref, cc_ref, ch_ref, w_ref, bias_ref, gn_ref, out_ref, u_ref, *, seq, gdim):
    pad = SUBLANES
    zeros = jnp.zeros((pad, LANES), F32)
    u_ref[0:pad, :] = zeros
    u_ref[pad + seq:pad + seq + pad, :] = zeros
    rt = _tile(seq, 256, SUBLANES)

    def mk(i, carry):
        rows = pl.ds(pl.multiple_of(i * rt, rt), rt)
        u_ref[pl.ds(pl.multiple_of(i * rt, rt) + pad, rt), :] = cc_ref[rows, :] * ch_ref[rows, :]
        return carry

    lax.fori_loop(0, seq // rt, mk, 0)

    gi = lax.broadcasted_iota(jnp.int32, (LANES, LANES), 0) // gdim
    gj = lax.broadcasted_iota(jnp.int32, (LANES, LANES), 1) // gdim
    gmat = jnp.where(gi == gj, 1.0, 0.0).astype(BF16)
    ridx = lax.broadcasted_iota(jnp.int32, (rt, LANES), 0)
    w0, w1, w2 = w_ref[0:1, :], w_ref[1:2, :], w_ref[2:3, :]

    def body(i, carry):
        r0 = pl.multiple_of(i * rt, rt)
        cur = u_ref[pl.ds(r0 + pad, rt), :]
        before = u_ref[pl.ds(r0, pad), :][pad - 1:pad, :]
        after = u_ref[pl.ds(r0 + pad + rt, pad), :][0:1, :]
        prev = jnp.where(ridx == 0, before, pltpu.roll(cur, 1, 0))
        nxt = jnp.where(ridx == rt - 1, after, pltpu.roll(cur, rt - 1, 0))
        y = prev * w0 + cur * w1 + nxt * w2 + bias_ref[...]
        o = cb_ref[pl.ds(r0, rt), :] * y
        sq = o * o
        hi = sq.astype(BF16)
        lo = (sq - hi.astype(F32)).astype(BF16)
        ssum = jnp.dot(hi, gmat, preferred_element_type=F32) + jnp.dot(lo, gmat, preferred_element_type=F32)
        out_ref[pl.ds(r0, rt), :] = (o * lax.rsqrt(ssum * (1.0 / gdim) + RMS_EPS) * gn_ref[...]).astype(out_ref.dtype)
        return carry

    lax.fori_loop(0, seq // rt, body, 0)


def _conv(z, conv_w, conv_b, gn, *, nseq, seq, d_c, cb_b, cb_c, cb_h):
    ncb = d_c // LANES

    def zspec(cb):
        return pl.BlockSpec((seq, LANES), lambda b, j, cb=cb: (b, cb + j))

    return pl.pallas_call(
        functools.partial(_conv_kernel, seq=seq, gdim=d_c // C_GROUPS),
        grid=(nseq, ncb),
        in_specs=[zspec(cb_b), zspec(cb_c), zspec(cb_h),
                  pl.BlockSpec((3, LANES), lambda b, j: (0, j)),
                  pl.BlockSpec((1, LANES), lambda b, j: (0, j)),
                  pl.BlockSpec((1, LANES), lambda b, j: (0, j))],
        out_specs=pl.BlockSpec((seq, LANES), lambda b, j: (b, j)),
        out_shape=jax.ShapeDtypeStruct((nseq * seq, d_c), BF16),
        scratch_shapes=[pltpu.VMEM((seq + 2 * SUBLANES, LANES), F32)],
        compiler_params=_params("parallel", "parallel"),
        name="short_conv",
    )(z, z, z, conv_w, conv_b, gn)


def _xattn_kernel(q_ref, k_ref, v_ref, out_ref, *, hd, scale):
    outs = []
    for h in range(X_HEADS):
        sl = slice(h * hd, (h + 1) * hd)
        s = lax.dot_general(q_ref[:, sl], k_ref[:, sl], NT_DIMS, preferred_element_type=F32) * scale
        m = jnp.max(s, axis=-1, keepdims=True)
        p = jnp.exp(s - m)
        p = p / jnp.sum(p, axis=-1, keepdims=True)
        outs.append(jnp.dot(p.astype(BF16), v_ref[:, sl], preferred_element_type=F32))
    out_ref[...] = jnp.concatenate(outs, axis=1).astype(out_ref.dtype)


def _xattn(qx, kvx, *, nseq, seq, n_mem, xw, tm=512):
    tm = _tile(seq, tm, SUBLANES)
    nt = seq // tm
    hd = xw // X_HEADS
    return pl.pallas_call(
        functools.partial(_xattn_kernel, hd=hd, scale=float(hd ** -0.5)),
        grid=(nseq, nt),
        in_specs=[pl.BlockSpec((tm, xw), lambda b, i: (b * nt + i, 0)),
                  pl.BlockSpec((n_mem, xw), lambda b, i: (b, 0)),
                  pl.BlockSpec((n_mem, xw), lambda b, i: (b, 1))],
        out_specs=pl.BlockSpec((tm, xw), lambda b, i: (b * nt + i, 0)),
        out_shape=jax.ShapeDtypeStruct((nseq * seq, xw), BF16),
        compiler_params=_params("parallel", "parallel"),
        name="xattn",
    )(qx, kvx, kvx)


def _router_kernel(x_ref, g_ref, wh_ref, wl_ref, bias_ref, eid_ref, ew_ref):
    xn = _rms(x_ref[...], g_ref[...])
    xh = xn.astype(BF16)
    xl = (xn - xh.astype(F32)).astype(BF16)
    logits = (jnp.dot(xh, wh_ref[...], preferred_element_type=F32)
              + jnp.dot(xl, wh_ref[...], preferred_element_type=F32)
              + jnp.dot(xh, wl_ref[...], preferred_element_type=F32)) + bias_ref[...]
    lane = lax.broadcasted_iota(jnp.int32, logits.shape, 1)
    ninf = -jnp.inf
    glog = jnp.where(lane < N_GROUPS, logits, ninf)
    gmax = jnp.max(glog, axis=-1, keepdims=True)
    gsum = jnp.sum(jnp.exp(glog - gmax), axis=-1, keepdims=True)
    g_val = 1.0 / gsum
    g_idx = jnp.min(jnp.where(glog == gmax, lane, ROUTER_LANES), axis=-1, keepdims=True)
    lo = N_GROUPS + g_idx * EXP_PER_GROUP
    elog = jnp.where(lane >= lo, jnp.where(lane < lo + EXP_PER_GROUP, logits, ninf), ninf)
    m1 = jnp.max(elog, axis=-1, keepdims=True)
    i1 = jnp.min(jnp.where(elog == m1, lane, ROUTER_LANES), axis=-1, keepdims=True)
    elog2 = jnp.where(lane == i1, ninf, elog)
    m2 = jnp.max(elog2, axis=-1, keepdims=True)
    i2 = jnp.min(jnp.where(elog2 == m2, lane, ROUTER_LANES), axis=-1, keepdims=True)
    e2 = jnp.exp(m2 - m1)
    w1 = g_val / (1.0 + e2)
    w2 = g_val * e2 / (1.0 + e2)
    eid_ref[...] = jnp.where(lane == 0, i1 - N_GROUPS, jnp.where(lane == 1, i2 - N_GROUPS, 0))
    ew_ref[...] = jnp.where(lane == 0, w1, jnp.where(lane == 1, w2, 0.0))


def _router(x, gain, wh, wl, bias, *, tm=256):
    m, d = x.shape
    tm = _tile(m, tm, SUBLANES)
    return pl.pallas_call(
        _router_kernel,
        grid=(m // tm,),
        in_specs=[pl.BlockSpec((tm, d), lambda i: (i, 0)),
                  pl.BlockSpec((1, d), lambda i: (0, 0)),
                  pl.BlockSpec((d, ROUTER_LANES), lambda i: (0, 0)),
                  pl.BlockSpec((d, ROUTER_LANES), lambda i: (0, 0)),
                  pl.BlockSpec((1, ROUTER_LANES), lambda i: (0, 0))],
        out_specs=[pl.BlockSpec((tm, ROUTER_LANES), lambda i: (i, 0)),
                   pl.BlockSpec((tm, ROUTER_LANES), lambda i: (i, 0))],
        out_shape=[jax.ShapeDtypeStruct((m, ROUTER_LANES), jnp.int32),
                   jax.ShapeDtypeStruct((m, ROUTER_LANES), F32)],
        compiler_params=_params("parallel"),
        name="router",
    )(x, gain, wh, wl, bias)


def _row_copy(src_hbm, src_row, dst, dst_row, sem):
    return pltpu.make_async_copy(src_hbm.at[pl.ds(src_row, 1)], dst.at[pl.ds(dst_row, 1)], sem)


def _gather_kernel(tok_ref, x_hbm, out_hbm, sem, *, rows):
    base = pl.program_id(0) * rows

    def start(r, carry):
        _row_copy(x_hbm, tok_ref[0, 0, r], out_hbm, base + r, sem).start()
        return carry

    lax.fori_loop(0, rows, start, 0)

    def wait(r, carry):
        _row_copy(x_hbm, 0, out_hbm, base + r, sem).wait()
        return carry

    lax.fori_loop(0, rows, wait, 0)


def _gather_rows(x, row_tok, *, rows=512):
    n_rows = row_tok.shape[0]
    rows = _tile(n_rows, rows, SUBLANES)
    nb = n_rows // rows
    return pl.pallas_call(
        functools.partial(_gather_kernel, rows=rows),
        grid=(nb,),
        in_specs=[pl.BlockSpec((1, 1, rows), lambda i: (i, 0, 0), memory_space=pltpu.SMEM),
                  pl.BlockSpec(memory_space=pl.ANY)],
        out_specs=pl.BlockSpec(memory_space=pl.ANY),
        out_shape=jax.ShapeDtypeStruct((n_rows, x.shape[1]), x.dtype),
        scratch_shapes=[pltpu.SemaphoreType.DMA(())],
        compiler_params=_params("arbitrary"),
        name="moe_gather",
    )(row_tok.reshape(nb, 1, rows), x)


def _moe_up_kernel(be_ref, nu_ref, xs_ref, g_ref, wg_ref, wu_ref, h_ref, wgb_ref, wub_ref):
    i = pl.program_id(1)
    prev = be_ref[jnp.maximum(i - 1, 0)]

    @pl.when(jnp.logical_or(i == 0, be_ref[i] != prev))
    def _():
        wgb_ref[...] = wg_ref[...].astype(BF16)
        wub_ref[...] = wu_ref[...].astype(BF16)

    @pl.when(i < nu_ref[0])
    def _():
        xn = _rms(xs_ref[...], g_ref[...]).astype(BF16)
        a = jnp.dot(xn, wgb_ref[...], preferred_element_type=F32)
        b = jnp.dot(xn, wub_ref[...], preferred_element_type=F32)
        h_ref[...] = (a * _sigmoid(a) * b).astype(h_ref.dtype)

    @pl.when(i >= nu_ref[0])
    def _():
        h_ref[...] = jnp.zeros_like(h_ref)


def _moe_up(xs, gain, w_gate, w_up, layer, blk_exp, n_used, *, bm, tn=256):
    n_rows, d = xs.shape
    de = w_gate.shape[-1]
    tn = _tile(de, tn)
    nb = n_rows // bm

    def row_map(j, i, be, nu):
        return (jnp.minimum(i, nu[0] - 1), 0)

    def w_map(j, i, be, nu):
        return (layer, be[i], 0, j)

    grid_spec = pltpu.PrefetchScalarGridSpec(
        num_scalar_prefetch=2,
        grid=(de // tn, nb),
        in_specs=[pl.BlockSpec((bm, d), row_map),
                  pl.BlockSpec((1, d), lambda j, i, be, nu: (0, 0)),
                  pl.BlockSpec((None, None, d, tn), w_map),
                  pl.BlockSpec((None, None, d, tn), w_map)],
        out_specs=pl.BlockSpec((bm, tn), lambda j, i, be, nu: (i, j)),
        scratch_shapes=[pltpu.VMEM((d, tn), BF16), pltpu.VMEM((d, tn), BF16)],
    )
    return pl.pallas_call(
        _moe_up_kernel,
        grid_spec=grid_spec,
        out_shape=jax.ShapeDtypeStruct((n_rows, de), BF16),
        compiler_params=_params("arbitrary", "arbitrary"),
        name="moe_up",
    )(blk_exp, n_used, xs, gain, w_gate, w_up)


def _moe_down_kernel(be_ref, nu_ref, h_ref, wd_ref, y_ref, wdb_ref):
    i = pl.program_id(1)
    prev = be_ref[jnp.maximum(i - 1, 0)]

    @pl.when(jnp.logical_or(i == 0, be_ref[i] != prev))
    def _():
        wdb_ref[...] = wd_ref[...].astype(BF16)

    @pl.when(i < nu_ref[0])
    def _():
        y_ref[...] = jnp.dot(h_ref[...], wdb_ref[...], preferred_element_type=F32)

    @pl.when(i >= nu_ref[0])
    def _():
        y_ref[...] = jnp.zeros_like(y_ref)


def _moe_down(hs, w_down, layer, blk_exp, n_used, *, bm, tn=1024):
    n_rows, de = hs.shape
    d = w_down.shape[-1]
    tn = _tile(d, tn)
    nb = n_rows // bm
    grid_spec = pltpu.PrefetchScalarGridSpec(
        num_scalar_prefetch=2,
        grid=(d // tn, nb),
        in_specs=[pl.BlockSpec((bm, de), lambda j, i, be, nu: (jnp.minimum(i, nu[0] - 1), 0)),
                  pl.BlockSpec((None, None, de, tn), lambda j, i, be, nu: (layer, be[i], 0, j))],
        out_specs=pl.BlockSpec((bm, tn), lambda j, i, be, nu: (i, j)),
        scratch_shapes=[pltpu.VMEM((de, tn), BF16)],
    )
    return pl.pallas_call(
        _moe_down_kernel,
        grid_spec=grid_spec,
        out_shape=jax.ShapeDtypeStruct((n_rows, d), F32),
        compiler_params=_params("arbitrary", "arbitrary"),
        name="moe_down",
    )(blk_exp, n_used, hs, w_down)


def _combine_kernel(pos_ref, x_ref, ew_ref, g_ref, ys_hbm, out_ref, ybuf, sem, *, tm, final):
    def start(r, carry):
        for k in range(TOP_K):
            _row_copy(ys_hbm, pos_ref[0, 0, TOP_K * r + k], ybuf.at[k], r, sem).start()
        return carry

    lax.fori_loop(0, tm, start, 0)

    def wait(r, carry):
        for k in range(TOP_K):
            _row_copy(ys_hbm, 0, ybuf.at[k], r, sem).wait()
        return carry

    lax.fori_loop(0, tm, wait, 0)
    w = ew_ref[...]
    y = x_ref[...]
    for k in range(TOP_K):
        y = y + w[:, k:k + 1] * ybuf[k]
    if final:
        y = _rms(y, g_ref[...])
    out_ref[...] = y


def _combine(x, ew, pos, ys, gain, *, final, tm=256):
    m, d = x.shape
    tm = _tile(m, tm, SUBLANES)
    nb = m // tm
    return pl.pallas_call(
        functools.partial(_combine_kernel, tm=tm, final=final),
        grid=(nb,),
        in_specs=[pl.BlockSpec((1, 1, TOP_K * tm), lambda i: (i, 0, 0), memory_space=pltpu.SMEM),
                  pl.BlockSpec((tm, d), lambda i: (i, 0)),
                  pl.BlockSpec((tm, ROUTER_LANES), lambda i: (i, 0)),
                  pl.BlockSpec((1, d), lambda i: (0, 0)),
                  pl.BlockSpec(memory_space=pl.ANY)],
        out_specs=pl.BlockSpec((tm, d), lambda i: (i, 0)),
        out_shape=jax.ShapeDtypeStruct((m, d), F32),
        scratch_shapes=[pltpu.VMEM((TOP_K, tm, d), F32), pltpu.SemaphoreType.DMA(())],
        compiler_params=_params("arbitrary"),
        name="moe_combine",
    )(pos.reshape(nb, 1, TOP_K * tm), x, ew, gain, ys)


def _dispatch_plan(eid, bm):
    n_pairs = eid.shape[0] * TOP_K
    flat_e = eid.reshape(-1)
    order = jnp.argsort(flat_e, stable=True).astype(jnp.int32)
    se = flat_e[order]
    bounds = jnp.searchsorted(se, jnp.arange(N_EXPERTS + 1, dtype=jnp.int32), side="left").astype(jnp.int32)
    start, counts = bounds[:-1], bounds[1:] - bounds[:-1]
    pcounts = (counts + bm - 1) // bm * bm
    pend = jnp.cumsum(pcounts)
    pstart = pend - pcounts
    dest = (pstart[se] + jnp.arange(n_pairs, dtype=jnp.int32) - start[se]).astype(jnp.int32)
    n_rows = n_pairs + N_EXPERTS * bm
    row_tok = jnp.zeros((n_rows,), jnp.int32).at[dest].set(order // TOP_K)
    pos = jnp.zeros((n_pairs,), jnp.int32).at[order].set(dest)
    blk_exp = jnp.minimum(
        jnp.searchsorted(pend, jnp.arange(n_rows // bm, dtype=jnp.int32) * bm, side="right"), N_EXPERTS - 1
    ).astype(jnp.int32)
    n_used = (pend[-1:] // bm).astype(jnp.int32)
    return row_tok, pos, blk_exp, n_used


def kernel(x_prompt, x_sample, mem_prompt, mem_sample, g_mix, w_in, a_lb_fwd, a_lb_bwd, a_norm, b_g_cq, b_w_uq, b_g_ckv, b_w_ukv, b_norm, c_conv_w, c_conv_b, c_norm, w_out, g_xattn, g_mem, w_xq, w_xk, w_xv, w_xo, g_ffn, w_rg, b_rg, w_re, b_re, w_gate, w_up, w_down, g_final):
    depth, d, _ = w_in.shape
    seq = x_prompt.shape[1]
    assert x_sample.shape[1] == seq and seq % CHUNK == 0
    n_mem = mem_prompt.shape[1]
    nb_p, nb_s = x_prompt.shape[0], x_sample.shape[0]
    nseq = nb_p + nb_s
    d_ak = a_lb_fwd.shape[1]
    a_heads = d_ak // LANES
    d_a = a_heads * A_HEAD_DIM
    q_lora = b_w_uq.shape[1]
    kv_lora = b_w_ukv.shape[1]
    b_heads = b_w_uq.shape[2] // (B_NOPE + B_ROPE)
    d_b = b_heads * B_V
    d_c = c_conv_b.shape[1]
    xw = w_xq.shape[2]
    assert d_a + d_b + d_c == d

    x = jnp.concatenate([x_prompt.reshape(-1, d), x_sample.reshape(-1, d)], axis=0)
    mem = jnp.concatenate([mem_prompt.reshape(-1, d), mem_sample.reshape(-1, d)], axis=0)
    n_tok = x.shape[0]

    sizes = (d_ak, d_ak, d_ak, d_a, d_a, q_lora, kv_lora, B_ROPE, d_c, d_c, d_c)
    offs = [0]
    for s in sizes:
        offs.append(offs[-1] + s)
    kr_pad = LANES - B_ROPE
    zw_used = offs[-1] + kr_pad
    zw = _round_up(zw_used, MXU_DIM)
    w_in_z = jnp.concatenate(
        [w_in[:, :, :offs[8]], jnp.zeros((depth, d, kr_pad), w_in.dtype), w_in[:, :, offs[8]:],
         jnp.zeros((depth, d, zw - zw_used), w_in.dtype)], axis=2).astype(BF16)
    zoff = {"a_q": offs[0], "a_ff": offs[1], "a_fb": offs[2], "a_i": offs[3], "a_g": offs[4], "b_cq": offs[5],
            "b_ckv": offs[6], "b_kr": offs[7], "c_b": offs[8] + kr_pad, "c_c": offs[9] + kr_pad,
            "c_h": offs[10] + kr_pad}
    assert all(v % LANES == 0 for v in zoff.values())
    assert zoff["b_cq"] % q_lora == 0 and zoff["b_ckv"] % kv_lora == 0

    qk_pad = 2 * LANES - (B_NOPE + B_ROPE)
    w_uq_z = jnp.pad(b_w_uq.reshape(depth, q_lora, b_heads, B_NOPE + B_ROPE),
                     ((0, 0), (0, 0), (0, 0), (0, qk_pad))).reshape(depth, q_lora, b_heads * 2 * LANES).astype(BF16)
    w_ukv_z = b_w_ukv.reshape(depth, kv_lora, b_heads, 2, B_V).transpose(0, 1, 3, 2, 4).reshape(
        depth, kv_lora, 2 * d_b).astype(BF16)
    w_out_z = w_out.astype(BF16)
    w_xq_z = w_xq.astype(BF16)
    w_xkv_z = jnp.concatenate([w_xk, w_xv], axis=2).astype(BF16)
    w_xo_z = w_xo.astype(BF16)
    w_r = jnp.concatenate([w_rg, w_re, jnp.zeros((depth, d, ROUTER_LANES - N_GROUPS - N_EXPERTS), F32)], axis=2)
    w_r_hi = w_r.astype(BF16)
    w_r_lo = (w_r - w_r_hi.astype(F32)).astype(BF16)
    b_r = jnp.concatenate([b_rg, b_re, jnp.zeros((depth, ROUTER_LANES - N_GROUPS - N_EXPERTS), F32)], axis=1)

    def lower_bounds(p):
        lb = jnp.cumsum(jax.nn.softmax(p.astype(F32), axis=0), axis=0)
        lb = lb - lb[0]
        return jnp.stack([jnp.log(lb), jnp.log1p(-lb), 1.0 - lb], axis=1)

    lbp = jnp.concatenate([lower_bounds(a_lb_fwd), lower_bounds(a_lb_bwd)], axis=1)

    inv = 1.0 / (ROPE_THETA ** (jnp.arange(0, B_ROPE, 2, dtype=F32) / B_ROPE))
    ang = jnp.arange(seq, dtype=F32)[:, None] * inv[None, :]
    zpad = jnp.zeros((seq, LANES - B_ROPE), F32)
    rope_c = jnp.concatenate([jnp.cos(ang), jnp.cos(ang), zpad], axis=1)
    rope_s = jnp.concatenate([jnp.sin(ang), jnp.sin(ang), zpad], axis=1)

    bm = 256 if n_tok * TOP_K >= N_EXPERTS * 256 else 16

    for l in range(depth):
        z = _matmul([(x, d, 0, 0)], w_in_z[l], gain=g_mix[l], out_dtype=F32, tm=512, tn=768, name="in_proj")
        o_a = _hgrn(z, lbp[l], a_norm[l].reshape(1, d_a), nseq=nseq, seq=seq, heads=a_heads,
                    cb_q=zoff["a_q"] // LANES, cb_ff=zoff["a_ff"] // LANES, cb_fb=zoff["a_fb"] // LANES,
                    cb_v=zoff["a_i"] // LANES, cb_g=zoff["a_g"] // LANES)
        q_raw = _matmul([(z, q_lora, zoff["b_cq"] // q_lora, 0)], w_uq_z[l], gain=b_g_cq[l], out_dtype=BF16,
                        tm=1024, tn=1024, name="q_up")
        kv_raw = _matmul([(z, kv_lora, zoff["b_ckv"] // kv_lora, 0)], w_ukv_z[l], gain=b_g_ckv[l],
                         out_dtype=BF16, tm=1024, tn=1024, name="kv_up")
        o_b = _mla(q_raw, kv_raw, z, rope_c, rope_s, b_norm[l].reshape(1, d_b), nseq=nseq, seq=seq,
                   heads=b_heads, cb_kr=zoff["b_kr"] // LANES)
        o_c = _conv(z, c_conv_w[l], c_conv_b[l].reshape(1, d_c), c_norm[l].reshape(1, d_c), nseq=nseq, seq=seq,
                    d_c=d_c, cb_b=zoff["c_b"] // LANES, cb_c=zoff["c_c"] // LANES, cb_h=zoff["c_h"] // LANES)
        kq = d_a
        assert d_b % kq == 0 and d_c == kq
        mix = [(o_a, kq, 0, 0)] + [(o_b, kq, i, 1 + i) for i in range(d_b // kq)] + [(o_c, kq, 0, 1 + d_b // kq)]
        x = _matmul(mix, w_out_z[l], res=x, out_dtype=F32, tm=512, tn=1024, name="out_proj")

        qx = _matmul([(x, d, 0, 0)], w_xq_z[l], gain=g_xattn[l], out_dtype=BF16, tm=512, tn=1024, name="xq_proj")
        kvx = _matmul([(mem, d, 0, 0)], w_xkv_z[l], gain=g_mem[l], out_dtype=BF16, tm=512, tn=1024, name="xkv_proj")
        ox = _xattn(qx, kvx, nseq=nseq, seq=seq, n_mem=n_mem, xw=xw)
        x = _matmul([(ox, xw, 0, 0)], w_xo_z[l], res=x, out_dtype=F32, tm=512, tn=1024, name="xo_proj")

        gain_f = g_ffn[l].reshape(1, d).astype(F32)
        eid, ew = _router(x, gain_f, w_r_hi[l], w_r_lo[l], b_r[l].reshape(1, ROUTER_LANES))
        row_tok, pos, blk_exp, n_used = _dispatch_plan(eid[:, :TOP_K], bm)
        xs = _gather_rows(x, row_tok)
        hs = _moe_up(xs, gain_f, w_gate, w_up, l, blk_exp, n_used, bm=bm)
        ys = _moe_down(hs, w_down, l, blk_exp, n_used, bm=bm)
        x = _combine(x, ew, pos, ys, g_final.reshape(1, d).astype(F32), final=(l == depth - 1))

    y = x.reshape(nseq, seq, d)
    return (y[:nb_p], y[nb_p:])
```

```python
import functools

import jax
import jax.numpy as jnp
from jax import lax
from jax.experimental import pallas as pl
from jax.experimental.pallas import tpu as pltpu

F32 = jnp.float32
BF16 = jnp.bfloat16

RMS_EPS = 1e-6
LANES = 128
SUBLANES = 8
MXU_DIM = 256
VMEM_LIMIT = 56 * 1024 * 1024

A_HEAD_DIM = 128
CHUNK = 64
B_V = 128
B_NOPE = 128
B_ROPE = 64
ROPE_THETA = 10000.0
C_GROUPS = 16
X_HEADS = 4
N_GROUPS = 8
EXP_PER_GROUP = 8
N_EXPERTS = N_GROUPS * EXP_PER_GROUP
TOP_K = 2
ROUTER_LANES = 128
LOG2E = 1.4426950408889634

NT_DIMS = (((1,), (1,)), ((), ()))
TN_DIMS = (((0,), (0,)), ((), ()))


def _tile(n, pref, align=LANES):
    if n <= pref:
        return n
    t = (pref // align) * align
    while t >= align:
        if n % t == 0:
            return t
        t -= align
    return n


def _round_up(n, m):
    return (n + m - 1) // m * m


def _params(*sem):
    return pltpu.CompilerParams(dimension_semantics=sem, vmem_limit_bytes=VMEM_LIMIT)


def _sigmoid(x):
    return 1.0 / (1.0 + jnp.exp(-x))


def _rms(x, gain):
    ms = jnp.mean(x * x, axis=-1, keepdims=True)
    return x * lax.rsqrt(ms + RMS_EPS) * gain


def _mm_kernel(*refs, n_lhs, normed, has_res):
    refs = list(refs)
    gain_ref = refs.pop(0) if normed else None
    lhs = refs[:n_lhs]
    ws = refs[n_lhs:2 * n_lhs]
    rest = refs[2 * n_lhs:]
    res_ref = rest.pop(0) if has_res else None
    out_ref = rest.pop(0)
    if normed:
        xn_ref = rest.pop(0)

        @pl.when(pl.program_id(1) == 0)
        def _():
            xn_ref[...] = _rms(lhs[0][...].astype(F32), gain_ref[...]).astype(BF16)

        acc = jnp.dot(xn_ref[...], ws[0][...], preferred_element_type=F32)
    else:
        acc = None
        for l_ref, w_ref in zip(lhs, ws):
            d = jnp.dot(l_ref[...].astype(BF16), w_ref[...], preferred_element_type=F32)
            acc = d if acc is None else acc + d
    if has_res:
        acc = acc + res_ref[...]
    out_ref[...] = acc.astype(out_ref.dtype)


def _matmul(lhs_list, w, *, gain=None, res=None, out_dtype=F32, tm=512, tn=512, name="mm"):
    m = lhs_list[0][0].shape[0]
    n = w.shape[1]
    tm = _tile(m, tm, SUBLANES)
    tn = _tile(n, tn)
    normed = gain is not None
    assert not normed or len(lhs_list) == 1
    in_specs, args = [], []
    if normed:
        k0 = lhs_list[0][1]
        in_specs.append(pl.BlockSpec((1, k0), lambda i, j: (0, 0)))
        args.append(gain.reshape(1, k0).astype(F32))
    for arr, k, cb, _ in lhs_list:
        in_specs.append(pl.BlockSpec((tm, k), lambda i, j, cb=cb: (i, cb)))
        args.append(arr)
    for _, k, _, rb in lhs_list:
        in_specs.append(pl.BlockSpec((k, tn), lambda i, j, rb=rb: (rb, j)))
        args.append(w)
    if res is not None:
        in_specs.append(pl.BlockSpec((tm, tn), lambda i, j: (i, j)))
        args.append(res)
    scratch = [pltpu.VMEM((tm, lhs_list[0][1]), BF16)] if normed else []
    return pl.pallas_call(
        functools.partial(_mm_kernel, n_lhs=len(lhs_list), normed=normed, has_res=res is not None),
        grid=(m // tm, n // tn),
        in_specs=in_specs,
        out_specs=pl.BlockSpec((tm, tn), lambda i, j: (i, j)),
        out_shape=jax.ShapeDtypeStruct((m, n), out_dtype),
        scratch_shapes=scratch,
        compiler_params=_params("parallel", "arbitrary"),
        name=name,
    )(*args)


def _gla_chunk(dirn, r0, q_ref, f_ref, v_ref, lbp_ref, st_ref, o_ref, bsc, ksc, msc, tri, ones):
    c = CHUNK
    fwd = dirn == 0
    rows = pl.ds(r0, c)
    x = f_ref[rows, :]
    e = jnp.exp(-jnp.abs(x))
    inv = 1.0 / (1.0 + e)
    log_sig = jnp.minimum(x, 0.0) - jnp.log1p(e)
    sig_neg = jnp.where(x >= 0, e * inv, inv)
    la = lbp_ref[3 * dirn:3 * dirn + 1, :]
    lc = lbp_ref[3 * dirn + 1:3 * dirn + 2, :] + log_sig
    logf = jnp.maximum(la, lc) + jnp.log1p(jnp.exp(-jnp.abs(la - lc)))
    kk = lbp_ref[3 * dirn + 2:3 * dirn + 3, :] * sig_neg

    hi = logf.astype(BF16)
    r1 = logf - hi.astype(F32)
    mid = r1.astype(BF16)
    lo = (r1 - mid.astype(F32)).astype(BF16)
    b = (jnp.dot(tri, hi, preferred_element_type=F32) + jnp.dot(tri, mid, preferred_element_type=F32)
         + jnp.dot(tri, lo, preferred_element_type=F32))

    qx = q_ref[rows, :]
    qs = qx * _sigmoid(qx)
    vv = v_ref[rows, :]
    vb = vv.astype(BF16)
    b_end = b[c - 1:c, :] if fwd else b[0:1, :]
    st = st_ref[dirn]
    o = lax.dot_general((qs * jnp.exp(b)).astype(BF16), st.astype(BF16), NT_DIMS, preferred_element_type=F32)

    bsc[dirn] = b
    ksc[dirn] = kk
    nblk = c // SUBLANES
    zero = jnp.zeros((SUBLANES, LANES), F32)

    def blk(a, i):
        return a[SUBLANES * i:SUBLANES * (i + 1)]

    def brow(r):
        return bsc[dirn, pl.ds(r, 1), :]

    q_anchor = [brow(SUBLANES * i if fwd else SUBLANES * i + SUBLANES - 1) for i in range(nblk)]
    k_anchor = [brow(SUBLANES * j + SUBLANES - 1 if fwd else SUBLANES * j) for j in range(nblk)]
    qh = [blk(qs, i) * jnp.exp(blk(b, i) - q_anchor[i]) for i in range(nblk)]
    kh = [blk(kk, j) * jnp.exp(k_anchor[j] - blk(b, j)) for j in range(nblk)]
    lhs_cols, rhs_cols = [], []
    for j in (range(nblk - 1) if fwd else range(1, nblk)):
        col = []
        for i in range(nblk):
            past = (j < i) if fwd else (j > i)
            col.append(qh[i] * jnp.exp(q_anchor[i] - k_anchor[j]) if past else zero)
        lhs_cols.append(jnp.concatenate(col, axis=0))
        rhs_cols.append(jnp.concatenate([kh[j] if r == j else zero for r in range(nblk)], axis=0))
    att = lax.dot_general(jnp.concatenate(lhs_cols, axis=1).astype(BF16), jnp.concatenate(rhs_cols, axis=1).astype(BF16),
                          NT_DIMS, preferred_element_type=F32)

    for i in range(nblk):
        for u in range(SUBLANES):
            s = SUBLANES * i + u
            prod = (blk(qs, i) * ksc[dirn, pl.ds(s, 1), :]) * jnp.exp(jnp.minimum(blk(b, i) - brow(s), 0.0))
            msc[dirn, SUBLANES * s:SUBLANES * (s + 1), :] = prod
    rs = jnp.dot(msc[dirn].astype(BF16), ones, preferred_element_type=F32)
    lane = lax.broadcasted_iota(jnp.int32, (SUBLANES, LANES), 1)
    sub = lax.broadcasted_iota(jnp.int32, (SUBLANES, LANES), 0)
    diag = []
    for i in range(nblk):
        dblk = zero
        for u in range(SUBLANES):
            s = SUBLANES * i + u
            dblk = jnp.where(lane == s, rs[SUBLANES * s:SUBLANES * (s + 1)], dblk)
        t = sub + SUBLANES * i
        diag.append(jnp.where((lane <= t) if fwd else (lane >= t), dblk, 0.0))
    att = att + jnp.concatenate(diag, axis=0)[:, :c]
    o = o + jnp.dot(att.astype(BF16), vb, preferred_element_type=F32)

    kt = kk * jnp.exp(b_end - b)
    st_ref[dirn] = st * jnp.exp(b_end) + lax.dot_general(vb, kt.astype(BF16), TN_DIMS, preferred_element_type=F32)
    o_ref[rows, :] = o


def _hgrn_kernel(q_ref, ff_ref, fb_ref, v_ref, g_ref, lbp_ref, gn_ref, out_ref,
                 of_ref, ob_ref, st_ref, bsc, ksc, msc, *, seq):
    c = CHUNK
    n = seq // c
    st_ref[...] = jnp.zeros_like(st_ref)
    row = lax.broadcasted_iota(jnp.int32, (c, c), 0)
    col = lax.broadcasted_iota(jnp.int32, (c, c), 1)
    tri_f = jnp.where(col <= row, 1.0, 0.0).astype(BF16)
    tri_b = jnp.where(col >= row, 1.0, 0.0).astype(BF16)
    ones = jnp.ones((LANES, LANES), BF16)

    def body(i, carry):
        _gla_chunk(0, pl.multiple_of(i * c, c), q_ref, ff_ref, v_ref, lbp_ref, st_ref, of_ref,
                   bsc, ksc, msc, tri_f, ones)
        _gla_chunk(1, pl.multiple_of((n - 1 - i) * c, c), q_ref, fb_ref, v_ref, lbp_ref, st_ref, ob_ref,
                   bsc, ksc, msc, tri_b, ones)
        return carry

    lax.fori_loop(0, n, body, 0)

    rt = _tile(seq, 256, SUBLANES)

    def fin(i, carry):
        rows = pl.ds(pl.multiple_of(i * rt, rt), rt)
        o = of_ref[rows, :] + ob_ref[rows, :]
        g = g_ref[rows, :]
        out_ref[rows, :] = (_rms(o, gn_ref[...]) * (g * _sigmoid(g))).astype(out_ref.dtype)
        return carry

    lax.fori_loop(0, seq // rt, fin, 0)


def _hgrn(z, lbp, gn, *, nseq, seq, heads, cb_q, cb_ff, cb_fb, cb_v, cb_g):
    def zspec(cb):
        return pl.BlockSpec((seq, LANES), lambda b, h, cb=cb: (b, cb + h))

    return pl.pallas_call(
        functools.partial(_hgrn_kernel, seq=seq),
        grid=(nseq, heads),
        in_specs=[zspec(cb_q), zspec(cb_ff), zspec(cb_fb), zspec(cb_v), zspec(cb_g),
                  pl.BlockSpec((6, LANES), lambda b, h: (0, h)),
                  pl.BlockSpec((1, LANES), lambda b, h: (0, h))],
        out_specs=pl.BlockSpec((seq, LANES), lambda b, h: (b, h)),
        out_shape=jax.ShapeDtypeStruct((nseq * seq, heads * LANES), BF16),
        scratch_shapes=[pltpu.VMEM((seq, LANES), F32), pltpu.VMEM((seq, LANES), F32),
                        pltpu.VMEM((2, LANES, LANES), F32),
                        pltpu.VMEM((2, CHUNK, LANES), F32), pltpu.VMEM((2, CHUNK, LANES), F32),
                        pltpu.VMEM((2, CHUNK * SUBLANES, LANES), F32)],
        compiler_params=_params("parallel", "parallel"),
        name="hgrn",
    )(z, z, z, z, z, lbp, gn)


def _rope128(x, ct, st):
    rot = pltpu.roll(x, 32, 1) - pltpu.roll(x, 96, 1)
    return x * ct + rot * st


def _mla_kernel(q_ref, kn_ref, v_ref, kr_ref, ctq_ref, stq_ref, ctk_ref, stk_ref, gn_ref, out_ref,
                kcat_ref, vcat_ref, *, seq, tk, n_sub, scale):
    qi = pl.program_id(2)
    rt = _tile(seq, 512, SUBLANES)

    @pl.when(qi == 0)
    def _():
        def build(i, carry):
            rows = pl.ds(pl.multiple_of(i * rt, rt), rt)
            krr = _rope128(kr_ref[rows, :], ctk_ref[rows, :], stk_ref[rows, :])
            kcat_ref[rows, 0:LANES] = kn_ref[rows, :]
            kcat_ref[rows, LANES:2 * LANES] = krr.astype(BF16)
            vcat_ref[rows, 0:LANES] = v_ref[rows, :]
            vcat_ref[rows, LANES:2 * LANES] = jnp.ones((rt, LANES), BF16)
            return carry

        lax.fori_loop(0, seq // rt, build, 0)

    ts = q_ref.shape[0] // n_sub
    qcs = []
    for a in range(n_sub):
        sl = slice(a * ts, (a + 1) * ts)
        q = q_ref[sl, :].astype(F32)
        qrr = _rope128(q[:, LANES:], ctq_ref[sl, :], stq_ref[sl, :])
        qcs.append((jnp.concatenate([q[:, :LANES], qrr], axis=1) * (scale * LOG2E)).astype(BF16))

    def body(kc, carry):
        rows = pl.ds(pl.multiple_of(kc * tk, tk), tk)
        kblk = kcat_ref[rows, :]
        vblk = vcat_ref[rows, :]
        out = []
        for a in range(n_sub):
            m, acc = carry[a]
            s = lax.dot_general(qcs[a], kblk, NT_DIMS, preferred_element_type=F32)
            m_new = jnp.maximum(m, jnp.max(s, axis=-1, keepdims=True))
            p = jnp.exp2(s - m_new)
            acc = jnp.exp2(m - m_new) * acc + jnp.dot(p.astype(BF16), vblk, preferred_element_type=F32)
            out.append((m_new, acc))
        return tuple(out)

    init = tuple((jnp.full((ts, 1), -jnp.inf, F32), jnp.zeros((ts, 2 * LANES), F32)) for _ in range(n_sub))
    res = lax.fori_loop(0, seq // tk, body, init)
    for a in range(n_sub):
        acc = res[a][1]
        o = acc[:, :LANES] / acc[:, LANES:]
        out_ref[a * ts:(a + 1) * ts, :] = _rms(o, gn_ref[...]).astype(out_ref.dtype)


def _mla(q_raw, kv_raw, z, ct, st, gn, *, nseq, seq, heads, cb_kr, tq=512, tk=512, n_sub=2):
    tq = _tile(seq, tq, SUBLANES)
    tk = _tile(seq, tk, SUBLANES)
    nq = seq // tq
    scale = float((B_NOPE + B_ROPE) ** -0.5)
    return pl.pallas_call(
        functools.partial(_mla_kernel, seq=seq, tk=tk, n_sub=n_sub, scale=scale),
        grid=(nseq, heads, nq),
        in_specs=[pl.BlockSpec((tq, 2 * LANES), lambda b, h, i: (b * nq + i, h)),
                  pl.BlockSpec((seq, LANES), lambda b, h, i: (b, h)),
                  pl.BlockSpec((seq, LANES), lambda b, h, i: (b, heads + h)),
                  pl.BlockSpec((seq, LANES), lambda b, h, i: (b, cb_kr)),
                  pl.BlockSpec((tq, LANES), lambda b, h, i: (i, 0)),
                  pl.BlockSpec((tq, LANES), lambda b, h, i: (i, 0)),
                  pl.BlockSpec((seq, LANES), lambda b, h, i: (0, 0)),
                  pl.BlockSpec((seq, LANES), lambda b, h, i: (0, 0)),
                  pl.BlockSpec((1, LANES), lambda b, h, i: (0, h))],
        out_specs=pl.BlockSpec((tq, LANES), lambda b, h, i: (b * nq + i, h)),
        out_shape=jax.ShapeDtypeStruct((nseq * seq, heads * B_V), BF16),
        scratch_shapes=[pltpu.VMEM((seq, 2 * LANES), BF16), pltpu.VMEM((seq, 2 * LANES), BF16)],
        compiler_params=_params("parallel", "parallel", "arbitrary"),
        name="mla_attn",
    )(q_raw, kv_raw, kv_raw, z, ct, st, ct, st, gn)


def _conv_kernel(cb_ref, cc_ref, ch_ref, w_ref, bias_ref, gn_ref, out_ref, u_ref, *, seq, gdim):
    pad = SUBLANES
    zeros = jnp.zeros((pad, LANES), F32)
    u_ref[0:pad, :] = zeros
    u_ref[pad + seq:pad + seq + pad, :] = zeros
    rt = _tile(seq, 256, SUBLANES)

    def mk(i, carry):
        rows = pl.ds(pl.multiple_of(i * rt, rt), rt)
        u_ref[pl.ds(pl.multiple_of(i * rt, rt) + pad, rt), :] = cc_ref[rows, :] * ch_ref[rows, :]
        return carry

    lax.fori_loop(0, seq // rt, mk, 0)

    gi = lax.broadcasted_iota(jnp.int32, (LANES, LANES), 0) // gdim
    gj = lax.broadcasted_iota(jnp.int32, (LANES, LANES), 1) // gdim
    gmat = jnp.where(gi == gj, 1.0, 0.0).astype(BF16)
    ridx = lax.broadcasted_iota(jnp.int32, (rt, LANES), 0)
    w0, w1, w2 = w_ref[0:1, :], w_ref[1:2, :], w_ref[2:3, :]

    def body(i, carry):
        r0 = pl.multiple_of(i * rt, rt)
        cur = u_ref[pl.ds(r0 + pad, rt), :]
        before = u_ref[pl.ds(r0, pad), :][pad - 1:pad, :]
        after = u_ref[pl.ds(r0 + pad + rt, pad), :][0:1, :]
        prev = jnp.where(ridx == 0, before, pltpu.roll(cur, 1, 0))
        nxt = jnp.where(ridx == rt - 1, after, pltpu.roll(cur, rt - 1, 0))
        y = prev * w0 + cur * w1 + nxt * w2 + bias_ref[...]
        o = cb_ref[pl.ds(r0, rt), :] * y
        sq = o * o
        hi = sq.astype(BF16)
        lo = (sq - hi.astype(F32)).astype(BF16)
        ssum = jnp.dot(hi, gmat, preferred_element_type=F32) + jnp.dot(lo, gmat, preferred_element_type=F32)
        out_ref[pl.ds(r0, rt), :] = (o * lax.rsqrt(ssum * (1.0 / gdim) + RMS_EPS) * gn_ref[...]).astype(out_ref.dtype)
        return carry

    lax.fori_loop(0, seq // rt, body, 0)


def _conv(z, conv_w, conv_b, gn, *, nseq, seq, d_c, cb_b, cb_c, cb_h):
    ncb = d_c // LANES

    def zspec(cb):
        return pl.BlockSpec((seq, LANES), lambda b, j, cb=cb: (b, cb + j))

    return pl.pallas_call(
        functools.partial(_conv_kernel, seq=seq, gdim=d_c // C_GROUPS),
        grid=(nseq, ncb),
        in_specs=[zspec(cb_b), zspec(cb_c), zspec(cb_h),
                  pl.BlockSpec((3, LANES), lambda b, j: (0, j)),
                  pl.BlockSpec((1, LANES), lambda b, j: (0, j)),
                  pl.BlockSpec((1, LANES), lambda b, j: (0, j))],
        out_specs=pl.BlockSpec((seq, LANES), lambda b, j: (b, j)),
        out_shape=jax.ShapeDtypeStruct((nseq * seq, d_c), BF16),
        scratch_shapes=[pltpu.VMEM((seq + 2 * SUBLANES, LANES), F32)],
        compiler_params=_params("parallel", "parallel"),
        name="short_conv",
    )(z, z, z, conv_w, conv_b, gn)


def _xattn_kernel(q_ref, k_ref, v_ref, out_ref, *, hd, scale):
    outs = []
    for h in range(X_HEADS):
        sl = slice(h * hd, (h + 1) * hd)
        s = lax.dot_general(q_ref[:, sl], k_ref[:, sl], NT_DIMS, preferred_element_type=F32) * scale
        m = jnp.max(s, axis=-1, keepdims=True)
        p = jnp.exp(s - m)
        p = p / jnp.sum(p, axis=-1, keepdims=True)
        outs.append(jnp.dot(p.astype(BF16), v_ref[:, sl], preferred_element_type=F32))
    out_ref[...] = jnp.concatenate(outs, axis=1).astype(out_ref.dtype)


def _xattn(qx, kvx, *, nseq, seq, n_mem, xw, tm=512):
    tm = _tile(seq, tm, SUBLANES)
    nt = seq // tm
    hd = xw // X_HEADS
    return pl.pallas_call(
        functools.partial(_xattn_kernel, hd=hd, scale=float(hd ** -0.5)),
        grid=(nseq, nt),
        in_specs=[pl.BlockSpec((tm, xw), lambda b, i: (b * nt + i, 0)),
                  pl.BlockSpec((n_mem, xw), lambda b, i: (b, 0)),
                  pl.BlockSpec((n_mem, xw), lambda b, i: (b, 1))],
        out_specs=pl.BlockSpec((tm, xw), lambda b, i: (b * nt + i, 0)),
        out_shape=jax.ShapeDtypeStruct((nseq * seq, xw), BF16),
        compiler_params=_params("parallel", "parallel"),
        name="xattn",
    )(qx, kvx, kvx)


def _router_kernel(x_ref, g_ref, wh_ref, wl_ref, bias_ref, eid_ref, ew_ref):
    xn = _rms(x_ref[...], g_ref[...])
    xh = xn.astype(BF16)
    xl = (xn - xh.astype(F32)).astype(BF16)
    logits = (jnp.dot(xh, wh_ref[...], preferred_element_type=F32)
              + jnp.dot(xl, wh_ref[...], preferred_element_type=F32)
              + jnp.dot(xh, wl_ref[...], preferred_element_type=F32)) + bias_ref[...]
    lane = lax.broadcasted_iota(jnp.int32, logits.shape, 1)
    ninf = -jnp.inf
    glog = jnp.where(lane < N_GROUPS, logits, ninf)
    gmax = jnp.max(glog, axis=-1, keepdims=True)
    gsum = jnp.sum(jnp.exp(glog - gmax), axis=-1, keepdims=True)
    g_val = 1.0 / gsum
    g_idx = jnp.min(jnp.where(glog == gmax, lane, ROUTER_LANES), axis=-1, keepdims=True)
    lo = N_GROUPS + g_idx * EXP_PER_GROUP
    elog = jnp.where(lane >= lo, jnp.where(lane < lo + EXP_PER_GROUP, logits, ninf), ninf)
    m1 = jnp.max(elog, axis=-1, keepdims=True)
    i1 = jnp.min(jnp.where(elog == m1, lane, ROUTER_LANES), axis=-1, keepdims=True)
    elog2 = jnp.where(lane == i1, ninf, elog)
    m2 = jnp.max(elog2, axis=-1, keepdims=True)
    i2 = jnp.min(jnp.where(elog2 == m2, lane, ROUTER_LANES), axis=-1, keepdims=True)
    e2 = jnp.exp(m2 - m1)
    w1 = g_val / (1.0 + e2)
    w2 = g_val * e2 / (1.0 + e2)
    eid_ref[...] = jnp.where(lane == 0, i1 - N_GROUPS, jnp.where(lane == 1, i2 - N_GROUPS, 0))
    ew_ref[...] = jnp.where(lane == 0, w1, jnp.where(lane == 1, w2, 0.0))


def _router(x, gain, wh, wl, bias, *, tm=256):
    m, d = x.shape
    tm = _tile(m, tm, SUBLANES)
    return pl.pallas_call(
        _router_kernel,
        grid=(m // tm,),
        in_specs=[pl.BlockSpec((tm, d), lambda i: (i, 0)),
                  pl.BlockSpec((1, d), lambda i: (0, 0)),
                  pl.BlockSpec((d, ROUTER_LANES), lambda i: (0, 0)),
                  pl.BlockSpec((d, ROUTER_LANES), lambda i: (0, 0)),
                  pl.BlockSpec((1, ROUTER_LANES), lambda i: (0, 0))],
        out_specs=[pl.BlockSpec((tm, ROUTER_LANES), lambda i: (i, 0)),
                   pl.BlockSpec((tm, ROUTER_LANES), lambda i: (i, 0))],
        out_shape=[jax.ShapeDtypeStruct((m, ROUTER_LANES), jnp.int32),
                   jax.ShapeDtypeStruct((m, ROUTER_LANES), F32)],
        compiler_params=_params("parallel"),
        name="router",
    )(x, gain, wh, wl, bias)


def _row_copy(src_hbm, src_row, dst, dst_row, sem):
    return pltpu.make_async_copy(src_hbm.at[pl.ds(src_row, 1)], dst.at[pl.ds(dst_row, 1)], sem)


def _gather_kernel(tok_ref, x_hbm, g_ref, out_ref, buf, sem, *, rows):
    def start(r, carry):
        _row_copy(x_hbm, tok_ref[0, 0, r], buf, r, sem).start()
        return carry

    lax.fori_loop(0, rows, start, 0)

    def wait(r, carry):
        _row_copy(x_hbm, 0, buf, r, sem).wait()
        return carry

    lax.fori_loop(0, rows, wait, 0)
    out_ref[...] = _rms(buf[...], g_ref[...]).astype(out_ref.dtype)


def _gather_rows(x, gain, row_tok, *, rows=256):
    n_rows = row_tok.shape[0]
    d = x.shape[1]
    rows = _tile(n_rows, rows, SUBLANES)
    nb = n_rows // rows
    return pl.pallas_call(
        functools.partial(_gather_kernel, rows=rows),
        grid=(nb,),
        in_specs=[pl.BlockSpec((1, 1, rows), lambda i: (i, 0, 0), memory_space=pltpu.SMEM),
                  pl.BlockSpec(memory_space=pl.ANY),
                  pl.BlockSpec((1, d), lambda i: (0, 0))],
        out_specs=pl.BlockSpec((rows, d), lambda i: (i, 0)),
        out_shape=jax.ShapeDtypeStruct((n_rows, d), BF16),
        scratch_shapes=[pltpu.VMEM((rows, d), F32), pltpu.SemaphoreType.DMA(())],
        compiler_params=_params("arbitrary"),
        name="moe_gather",
    )(row_tok.reshape(nb, 1, rows), x, gain)


def _moe_up_kernel(be_ref, nu_ref, xs_ref, wg_ref, wu_ref, h_ref, wgb_ref, wub_ref):
    i = pl.program_id(1)
    prev = be_ref[jnp.maximum(i - 1, 0)]

    @pl.when(jnp.logical_or(i == 0, be_ref[i] != prev))
    def _():
        wgb_ref[...] = wg_ref[...].astype(BF16)
        wub_ref[...] = wu_ref[...].astype(BF16)

    @pl.when(i < nu_ref[0])
    def _():
        xn = xs_ref[...]
        a = jnp.dot(xn, wgb_ref[...], preferred_element_type=F32)
        b = jnp.dot(xn, wub_ref[...], preferred_element_type=F32)
        h_ref[...] = (a * _sigmoid(a) * b).astype(h_ref.dtype)

    @pl.when(i >= nu_ref[0])
    def _():
        h_ref[...] = jnp.zeros_like(h_ref)


def _moe_up(xs, w_gate, w_up, layer, blk_exp, n_used, *, bm, tn=512):
    n_rows, d = xs.shape
    de = w_gate.shape[-1]
    tn = _tile(de, tn)
    nb = n_rows // bm

    def row_map(j, i, be, nu):
        return (jnp.minimum(i, nu[0] - 1), 0)

    def w_map(j, i, be, nu):
        return (layer, be[i], 0, j)

    grid_spec = pltpu.PrefetchScalarGridSpec(
        num_scalar_prefetch=2,
        grid=(de // tn, nb),
        in_specs=[pl.BlockSpec((bm, d), row_map),
                  pl.BlockSpec((None, None, d, tn), w_map),
                  pl.BlockSpec((None, None, d, tn), w_map)],
        out_specs=pl.BlockSpec((bm, tn), lambda j, i, be, nu: (i, j)),
        scratch_shapes=[pltpu.VMEM((d, tn), BF16), pltpu.VMEM((d, tn), BF16)],
    )
    return pl.pallas_call(
        _moe_up_kernel,
        grid_spec=grid_spec,
        out_shape=jax.ShapeDtypeStruct((n_rows, de), BF16),
        compiler_params=_params("arbitrary", "arbitrary"),
        name="moe_up",
    )(blk_exp, n_used, xs, w_gate, w_up)


def _moe_down_kernel(be_ref, nu_ref, h_ref, wd_ref, y_ref, wdb_ref):
    i = pl.program_id(1)
    prev = be_ref[jnp.maximum(i - 1, 0)]

    @pl.when(jnp.logical_or(i == 0, be_ref[i] != prev))
    def _():
        wdb_ref[...] = wd_ref[...].astype(BF16)

    @pl.when(i < nu_ref[0])
    def _():
        y_ref[...] = jnp.dot(h_ref[...], wdb_ref[...], preferred_element_type=F32)

    @pl.when(i >= nu_ref[0])
    def _():
        y_ref[...] = jnp.zeros_like(y_ref)


def _moe_down(hs, w_down, layer, blk_exp, n_used, *, bm, tn=1024):
    n_rows, de = hs.shape
    d = w_down.shape[-1]
    tn = _tile(d, tn)
    nb = n_rows // bm
    grid_spec = pltpu.PrefetchScalarGridSpec(
        num_scalar_prefetch=2,
        grid=(d // tn, nb),
        in_specs=[pl.BlockSpec((bm, de), lambda j, i, be, nu: (jnp.minimum(i, nu[0] - 1), 0)),
                  pl.BlockSpec((None, None, de, tn), lambda j, i, be, nu: (layer, be[i], 0, j))],
        out_specs=pl.BlockSpec((bm, tn), lambda j, i, be, nu: (i, j)),
        scratch_shapes=[pltpu.VMEM((de, tn), BF16)],
    )
    return pl.pallas_call(
        _moe_down_kernel,
        grid_spec=grid_spec,
        out_shape=jax.ShapeDtypeStruct((n_rows, d), F32),
        compiler_params=_params("arbitrary", "arbitrary"),
        name="moe_down",
    )(blk_exp, n_used, hs, w_down)


def _combine_kernel(pos_ref, x_ref, ew_ref, g_ref, ys_hbm, out_ref, ybuf, sem, *, tm, final):
    def start(r, carry):
        for k in range(TOP_K):
            _row_copy(ys_hbm, pos_ref[0, 0, TOP_K * r + k], ybuf.at[k], r, sem).start()
        return carry

    lax.fori_loop(0, tm, start, 0)

    def wait(r, carry):
        for k in range(TOP_K):
            _row_copy(ys_hbm, 0, ybuf.at[k], r, sem).wait()
        return carry

    lax.fori_loop(0, tm, wait, 0)
    w = ew_ref[...]
    y = x_ref[...]
    for k in range(TOP_K):
        y = y + w[:, k:k + 1] * ybuf[k]
    if final:
        y = _rms(y, g_ref[...])
    out_ref[...] = y


def _combine(x, ew, pos, ys, gain, *, final, tm=256):
    m, d = x.shape
    tm = _tile(m, tm, SUBLANES)
    nb = m // tm
    return pl.pallas_call(
        functools.partial(_combine_kernel, tm=tm, final=final),
        grid=(nb,),
        in_specs=[pl.BlockSpec((1, 1, TOP_K * tm), lambda i: (i, 0, 0), memory_space=pltpu.SMEM),
                  pl.BlockSpec((tm, d), lambda i: (i, 0)),
                  pl.BlockSpec((tm, ROUTER_LANES), lambda i: (i, 0)),
                  pl.BlockSpec((1, d), lambda i: (0, 0)),
                  pl.BlockSpec(memory_space=pl.ANY)],
        out_specs=pl.BlockSpec((tm, d), lambda i: (i, 0)),
        out_shape=jax.ShapeDtypeStruct((m, d), F32),
        scratch_shapes=[pltpu.VMEM((TOP_K, tm, d), F32), pltpu.SemaphoreType.DMA(())],
        compiler_params=_params("arbitrary"),
        name="moe_combine",
    )(pos.reshape(nb, 1, TOP_K * tm), x, ew, gain, ys)


def _dispatch_plan(eid, bm):
    n_pairs = eid.shape[0] * TOP_K
    flat_e = eid.reshape(-1)
    order = jnp.argsort(flat_e, stable=True).astype(jnp.int32)
    se = flat_e[order]
    bounds = jnp.searchsorted(se, jnp.arange(N_EXPERTS + 1, dtype=jnp.int32), side="left").astype(jnp.int32)
    start, counts = bounds[:-1], bounds[1:] - bounds[:-1]
    pcounts = (counts + bm - 1) // bm * bm
    pend = jnp.cumsum(pcounts)
    pstart = pend - pcounts
    dest = (pstart[se] + jnp.arange(n_pairs, dtype=jnp.int32) - start[se]).astype(jnp.int32)
    n_rows = n_pairs + N_EXPERTS * bm
    row_tok = jnp.zeros((n_rows,), jnp.int32).at[dest].set(order // TOP_K)
    pos = jnp.zeros((n_pairs,), jnp.int32).at[order].set(dest)
    blk_exp = jnp.minimum(
        jnp.searchsorted(pend, jnp.arange(n_rows // bm, dtype=jnp.int32) * bm, side="right"), N_EXPERTS - 1
    ).astype(jnp.int32)
    n_used = (pend[-1:] // bm).astype(jnp.int32)
    return row_tok, pos, blk_exp, n_used


def kernel(x_prompt, x_sample, mem_prompt, mem_sample, g_mix, w_in, a_lb_fwd, a_lb_bwd, a_norm, b_g_cq, b_w_uq, b_g_ckv, b_w_ukv, b_norm, c_conv_w, c_conv_b, c_norm, w_out, g_xattn, g_mem, w_xq, w_xk, w_xv, w_xo, g_ffn, w_rg, b_rg, w_re, b_re, w_gate, w_up, w_down, g_final):
    depth, d, _ = w_in.shape
    seq = x_prompt.shape[1]
    assert x_sample.shape[1] == seq and seq % CHUNK == 0
    n_mem = mem_prompt.shape[1]
    nb_p, nb_s = x_prompt.shape[0], x_sample.shape[0]
    nseq = nb_p + nb_s
    d_ak = a_lb_fwd.shape[1]
    a_heads = d_ak // LANES
    d_a = a_heads * A_HEAD_DIM
    q_lora = b_w_uq.shape[1]
    kv_lora = b_w_ukv.shape[1]
    b_heads = b_w_uq.shape[2] // (B_NOPE + B_ROPE)
    d_b = b_heads * B_V
    d_c = c_conv_b.shape[1]
    xw = w_xq.shape[2]
    assert d_a + d_b + d_c == d

    x = jnp.concatenate([x_prompt.reshape(-1, d), x_sample.reshape(-1, d)], axis=0)
    mem = jnp.concatenate([mem_prompt.reshape(-1, d), mem_sample.reshape(-1, d)], axis=0)
    n_tok = x.shape[0]

    sizes = (d_ak, d_ak, d_ak, d_a, d_a, q_lora, kv_lora, B_ROPE, d_c, d_c, d_c)
    offs = [0]
    for s in sizes:
        offs.append(offs[-1] + s)
    kr_pad = LANES - B_ROPE
    zw_used = offs[-1] + kr_pad
    zw = _round_up(zw_used, MXU_DIM)
    w_in_z = jnp.concatenate(
        [w_in[:, :, :offs[8]], jnp.zeros((depth, d, kr_pad), w_in.dtype), w_in[:, :, offs[8]:],
         jnp.zeros((depth, d, zw - zw_used), w_in.dtype)], axis=2).astype(BF16)
    zoff = {"a_q": offs[0], "a_ff": offs[1], "a_fb": offs[2], "a_i": offs[3], "a_g": offs[4], "b_cq": offs[5],
            "b_ckv": offs[6], "b_kr": offs[7], "c_b": offs[8] + kr_pad, "c_c": offs[9] + kr_pad,
            "c_h": offs[10] + kr_pad}
    assert all(v % LANES == 0 for v in zoff.values())
    assert zoff["b_cq"] % q_lora == 0 and zoff["b_ckv"] % kv_lora == 0

    qk_pad = 2 * LANES - (B_NOPE + B_ROPE)
    w_uq_z = jnp.pad(b_w_uq.reshape(depth, q_lora, b_heads, B_NOPE + B_ROPE),
                     ((0, 0), (0, 0), (0, 0), (0, qk_pad))).reshape(depth, q_lora, b_heads * 2 * LANES).astype(BF16)
    w_ukv_z = b_w_ukv.reshape(depth, kv_lora, b_heads, 2, B_V).transpose(0, 1, 3, 2, 4).reshape(
        depth, kv_lora, 2 * d_b).astype(BF16)
    w_out_z = w_out.astype(BF16)
    w_xq_z = w_xq.astype(BF16)
    w_xkv_z = jnp.concatenate([w_xk, w_xv], axis=2).astype(BF16)
    w_xo_z = w_xo.astype(BF16)
    w_r = jnp.concatenate([w_rg, w_re, jnp.zeros((depth, d, ROUTER_LANES - N_GROUPS - N_EXPERTS), F32)], axis=2)
    w_r_hi = w_r.astype(BF16)
    w_r_lo = (w_r - w_r_hi.astype(F32)).astype(BF16)
    b_r = jnp.concatenate([b_rg, b_re, jnp.zeros((depth, ROUTER_LANES - N_GROUPS - N_EXPERTS), F32)], axis=1)

    def lower_bounds(p):
        lb = jnp.cumsum(jax.nn.softmax(p.astype(F32), axis=0), axis=0)
        lb = lb - lb[0]
        return jnp.stack([jnp.log(lb), jnp.log1p(-lb), 1.0 - lb], axis=1)

    lbp = jnp.concatenate([lower_bounds(a_lb_fwd), lower_bounds(a_lb_bwd)], axis=1)

    inv = 1.0 / (ROPE_THETA ** (jnp.arange(0, B_ROPE, 2, dtype=F32) / B_ROPE))
    ang = jnp.arange(seq, dtype=F32)[:, None] * inv[None, :]
    zpad = jnp.zeros((seq, LANES - B_ROPE), F32)
    rope_c = jnp.concatenate([jnp.cos(ang), jnp.cos(ang), zpad], axis=1)
    rope_s = jnp.concatenate([jnp.sin(ang), jnp.sin(ang), zpad], axis=1)

    bm = 256 if n_tok * TOP_K >= N_EXPERTS * 256 else 16

    for l in range(depth):
        z = _matmul([(x, d, 0, 0)], w_in_z[l], gain=g_mix[l], out_dtype=F32, tm=512, tn=768, name="in_proj")
        o_a = _hgrn(z, lbp[l], a_norm[l].reshape(1, d_a), nseq=nseq, seq=seq, heads=a_heads,
                    cb_q=zoff["a_q"] // LANES, cb_ff=zoff["a_ff"] // LANES, cb_fb=zoff["a_fb"] // LANES,
                    cb_v=zoff["a_i"] // LANES, cb_g=zoff["a_g"] // LANES)
        q_raw = _matmul([(z, q_lora, zoff["b_cq"] // q_lora, 0)], w_uq_z[l], gain=b_g_cq[l], out_dtype=BF16,
                        tm=1024, tn=1024, name="q_up")
        kv_raw = _matmul([(z, kv_lora, zoff["b_ckv"] // kv_lora, 0)], w_ukv_z[l], gain=b_g_ckv[l],
                         out_dtype=BF16, tm=1024, tn=1024, name="kv_up")
        o_b = _mla(q_raw, kv_raw, z, rope_c, rope_s, b_norm[l].reshape(1, d_b), nseq=nseq, seq=seq,
                   heads=b_heads, cb_kr=zoff["b_kr"] // LANES)
        o_c = _conv(z, c_conv_w[l], c_conv_b[l].reshape(1, d_c), c_norm[l].reshape(1, d_c), nseq=nseq, seq=seq,
                    d_c=d_c, cb_b=zoff["c_b"] // LANES, cb_c=zoff["c_c"] // LANES, cb_h=zoff["c_h"] // LANES)
        kq = d_a
        assert d_b % kq == 0 and d_c == kq
        mix = [(o_a, kq, 0, 0)] + [(o_b, kq, i, 1 + i) for i in range(d_b // kq)] + [(o_c, kq, 0, 1 + d_b // kq)]
        x = _matmul(mix, w_out_z[l], res=x, out_dtype=F32, tm=512, tn=1024, name="out_proj")

        qx = _matmul([(x, d, 0, 0)], w_xq_z[l], gain=g_xattn[l], out_dtype=BF16, tm=512, tn=1024, name="xq_proj")
        kvx = _matmul([(mem, d, 0, 0)], w_xkv_z[l], gain=g_mem[l], out_dtype=BF16, tm=512, tn=1024, name="xkv_proj")
        ox = _xattn(qx, kvx, nseq=nseq, seq=seq, n_mem=n_mem, xw=xw)
        x = _matmul([(ox, xw, 0, 0)], w_xo_z[l], res=x, out_dtype=F32, tm=512, tn=1024, name="xo_proj")

        gain_f = g_ffn[l].reshape(1, d).astype(F32)
        eid, ew = _router(x, gain_f, w_r_hi[l], w_r_lo[l], b_r[l].reshape(1, ROUTER_LANES))
        row_tok, pos, blk_exp, n_used = _dispatch_plan(eid[:, :TOP_K], bm)
        xs = _gather_rows(x, gain_f, row_tok)
        hs = _moe_up(xs, w_gate, w_up, l, blk_exp, n_used, bm=bm)
        ys = _moe_down(hs, w_down, l, blk_exp, n_used, bm=bm)
        x = _combine(x, ew, pos, ys, g_final.reshape(1, d).astype(F32), final=(l == depth - 1))

    y = x.reshape(nseq, seq, d)
    return (y[:nb_p], y[nb_p:])
```

```python
import functools

import jax
import jax.numpy as jnp
from jax import lax
from jax.experimental import pallas as pl
from jax.experimental.pallas import tpu as pltpu

F32 = jnp.float32
BF16 = jnp.bfloat16

RMS_EPS = 1e-6
LANES = 128
SUBLANES = 8
MXU_DIM = 256
VMEM_LIMIT = 56 * 1024 * 1024

A_HEAD_DIM = 128
CHUNK = 64
B_V = 128
B_NOPE = 128
B_ROPE = 64
ROPE_THETA = 10000.0
C_GROUPS = 16
X_HEADS = 4
N_GROUPS = 8
EXP_PER_GROUP = 8
N_EXPERTS = N_GROUPS * EXP_PER_GROUP
TOP_K = 2
ROUTER_LANES = 128
LOG2E = 1.4426950408889634

NT_DIMS = (((1,), (1,)), ((), ()))
TN_DIMS = (((0,), (0,)), ((), ()))


def _tile(n, pref, align=LANES):
    if n <= pref:
        return n
    t = (pref // align) * align
    while t >= align:
        if n % t == 0:
            return t
        t -= align
    return n


def _round_up(n, m):
    return (n + m - 1) // m * m


def _params(*sem):
    return pltpu.CompilerParams(dimension_semantics=sem, vmem_limit_bytes=VMEM_LIMIT)


def _sigmoid(x):
    return 1.0 / (1.0 + jnp.exp(-x))


def _rms(x, gain):
    ms = jnp.mean(x * x, axis=-1, keepdims=True)
    return x * lax.rsqrt(ms + RMS_EPS) * gain


def _mm_kernel(*refs, n_lhs, normed, has_res):
    refs = list(refs)
    gain_ref = refs.pop(0) if normed else None
    lhs = refs[:n_lhs]
    ws = refs[n_lhs:2 * n_lhs]
    rest = refs[2 * n_lhs:]
    res_ref = rest.pop(0) if has_res else None
    out_ref = rest.pop(0)
    if normed:
        xn_ref = rest.pop(0)

        @pl.when(pl.program_id(1) == 0)
        def _():
            xn_ref[...] = _rms(lhs[0][...].astype(F32), gain_ref[...]).astype(BF16)

        acc = jnp.dot(xn_ref[...], ws[0][...], preferred_element_type=F32)
    else:
        acc = None
        for l_ref, w_ref in zip(lhs, ws):
            d = jnp.dot(l_ref[...].astype(BF16), w_ref[...], preferred_element_type=F32)
            acc = d if acc is None else acc + d
    if has_res:
        acc = acc + res_ref[...]
    out_ref[...] = acc.astype(out_ref.dtype)


def _matmul(lhs_list, w, *, gain=None, res=None, out_dtype=F32, tm=512, tn=512, name="mm"):
    m = lhs_list[0][0].shape[0]
    n = w.shape[1]
    tm = _tile(m, tm, SUBLANES)
    tn = _tile(n, tn)
    normed = gain is not None
    assert not normed or len(lhs_list) == 1
    in_specs, args = [], []
    if normed:
        k0 = lhs_list[0][1]
        in_specs.append(pl.BlockSpec((1, k0), lambda i, j: (0, 0)))
        args.append(gain.reshape(1, k0).astype(F32))
    for arr, k, cb, _ in lhs_list:
        in_specs.append(pl.BlockSpec((tm, k), lambda i, j, cb=cb: (i, cb)))
        args.append(arr)
    for _, k, _, rb in lhs_list:
        in_specs.append(pl.BlockSpec((k, tn), lambda i, j, rb=rb: (rb, j)))
        args.append(w)
    if res is not None:
        in_specs.append(pl.BlockSpec((tm, tn), lambda i, j: (i, j)))
        args.append(res)
    scratch = [pltpu.VMEM((tm, lhs_list[0][1]), BF16)] if normed else []
    return pl.pallas_call(
        functools.partial(_mm_kernel, n_lhs=len(lhs_list), normed=normed, has_res=res is not None),
        grid=(m // tm, n // tn),
        in_specs=in_specs,
        out_specs=pl.BlockSpec((tm, tn), lambda i, j: (i, j)),
        out_shape=jax.ShapeDtypeStruct((m, n), out_dtype),
        scratch_shapes=scratch,
        compiler_params=_params("parallel", "arbitrary"),
        name=name,
    )(*args)


def _gla_chunk(dirn, r0, q_ref, f_ref, v_ref, lbp_ref, st_ref, o_ref, bsc, ksc, msc, tri, ones):
    c = CHUNK
    fwd = dirn == 0
    rows = pl.ds(r0, c)
    x = f_ref[rows, :]
    e = jnp.exp(-jnp.abs(x))
    inv = 1.0 / (1.0 + e)
    log_sig = jnp.minimum(x, 0.0) - jnp.log1p(e)
    sig_neg = jnp.where(x >= 0, e * inv, inv)
    la = lbp_ref[3 * dirn:3 * dirn + 1, :]
    lc = lbp_ref[3 * dirn + 1:3 * dirn + 2, :] + log_sig
    logf = jnp.maximum(la, lc) + jnp.log1p(jnp.exp(-jnp.abs(la - lc)))
    kk = lbp_ref[3 * dirn + 2:3 * dirn + 3, :] * sig_neg

    hi = logf.astype(BF16)
    r1 = logf - hi.astype(F32)
    mid = r1.astype(BF16)
    lo = (r1 - mid.astype(F32)).astype(BF16)
    b = (jnp.dot(tri, hi, preferred_element_type=F32) + jnp.dot(tri, mid, preferred_element_type=F32)
         + jnp.dot(tri, lo, preferred_element_type=F32))

    qx = q_ref[rows, :]
    qs = qx * _sigmoid(qx)
    vv = v_ref[rows, :]
    vb = vv.astype(BF16)
    b_end = b[c - 1:c, :] if fwd else b[0:1, :]
    st = st_ref[...]
    o = lax.dot_general((qs * jnp.exp(b)).astype(BF16), st.astype(BF16), NT_DIMS, preferred_element_type=F32)

    bsc[...] = b
    ksc[...] = kk
    nblk = c // SUBLANES
    zero = jnp.zeros((SUBLANES, LANES), F32)

    def blk(a, i):
        return a[SUBLANES * i:SUBLANES * (i + 1)]

    def brow(r):
        return bsc[pl.ds(r, 1), :]

    q_anchor = [brow(SUBLANES * i if fwd else SUBLANES * i + SUBLANES - 1) for i in range(nblk)]
    k_anchor = [brow(SUBLANES * j + SUBLANES - 1 if fwd else SUBLANES * j) for j in range(nblk)]
    qh = [blk(qs, i) * jnp.exp(blk(b, i) - q_anchor[i]) for i in range(nblk)]
    kh = [blk(kk, j) * jnp.exp(k_anchor[j] - blk(b, j)) for j in range(nblk)]
    lhs_cols, rhs_cols = [], []
    for j in (range(nblk - 1) if fwd else range(1, nblk)):
        col = []
        for i in range(nblk):
            past = (j < i) if fwd else (j > i)
            col.append(qh[i] * jnp.exp(q_anchor[i] - k_anchor[j]) if past else zero)
        lhs_cols.append(jnp.concatenate(col, axis=0))
        rhs_cols.append(jnp.concatenate([kh[j] if r == j else zero for r in range(nblk)], axis=0))
    att = lax.dot_general(jnp.concatenate(lhs_cols, axis=1).astype(BF16), jnp.concatenate(rhs_cols, axis=1).astype(BF16),
                          NT_DIMS, preferred_element_type=F32)

    for i in range(nblk):
        for u in range(SUBLANES):
            s = SUBLANES * i + u
            prod = (blk(qs, i) * ksc[pl.ds(s, 1), :]) * jnp.exp(jnp.minimum(blk(b, i) - brow(s), 0.0))
            msc[SUBLANES * s:SUBLANES * (s + 1), :] = prod
    rs = jnp.dot(msc[...].astype(BF16), ones, preferred_element_type=F32)
    lane = lax.broadcasted_iota(jnp.int32, (SUBLANES, LANES), 1)
    sub = lax.broadcasted_iota(jnp.int32, (SUBLANES, LANES), 0)
    diag = []
    for i in range(nblk):
        dblk = zero
        for u in range(SUBLANES):
            s = SUBLANES * i + u
            dblk = jnp.where(lane == s, rs[SUBLANES * s:SUBLANES * (s + 1)], dblk)
        t = sub + SUBLANES * i
        diag.append(jnp.where((lane <= t) if fwd else (lane >= t), dblk, 0.0))
    att = att + jnp.concatenate(diag, axis=0)[:, :c]
    o = o + jnp.dot(att.astype(BF16), vb, preferred_element_type=F32)

    kt = kk * jnp.exp(b_end - b)
    st_ref[...] = st * jnp.exp(b_end) + lax.dot_general(vb, kt.astype(BF16), TN_DIMS, preferred_element_type=F32)
    o_ref[rows, :] = o


def _hgrn_kernel(q_ref, ff_ref, fb_ref, v_ref, g_ref, lbp_ref, gn_ref, out_ref,
                 of_ref, ob_ref, stf, bf, kf, mf, stb, bb, kb, mb, *, seq):
    c = CHUNK
    n = seq // c
    stf[...] = jnp.zeros_like(stf)
    stb[...] = jnp.zeros_like(stb)
    row = lax.broadcasted_iota(jnp.int32, (c, c), 0)
    col = lax.broadcasted_iota(jnp.int32, (c, c), 1)
    tri_f = jnp.where(col <= row, 1.0, 0.0).astype(BF16)
    tri_b = jnp.where(col >= row, 1.0, 0.0).astype(BF16)
    ones = jnp.ones((LANES, LANES), BF16)

    def body(i, carry):
        _gla_chunk(0, pl.multiple_of(i * c, c), q_ref, ff_ref, v_ref, lbp_ref, stf, of_ref,
                   bf, kf, mf, tri_f, ones)
        _gla_chunk(1, pl.multiple_of((n - 1 - i) * c, c), q_ref, fb_ref, v_ref, lbp_ref, stb, ob_ref,
                   bb, kb, mb, tri_b, ones)
        return carry

    lax.fori_loop(0, n, body, 0)

    rt = _tile(seq, 256, SUBLANES)

    def fin(i, carry):
        rows = pl.ds(pl.multiple_of(i * rt, rt), rt)
        o = of_ref[rows, :] + ob_ref[rows, :]
        g = g_ref[rows, :]
        out_ref[rows, :] = (_rms(o, gn_ref[...]) * (g * _sigmoid(g))).astype(out_ref.dtype)
        return carry

    lax.fori_loop(0, seq // rt, fin, 0)


def _hgrn(z, lbp, gn, *, nseq, seq, heads, cb_q, cb_ff, cb_fb, cb_v, cb_g):
    def zspec(cb):
        return pl.BlockSpec((seq, LANES), lambda b, h, cb=cb: (b, cb + h))

    return pl.pallas_call(
        functools.partial(_hgrn_kernel, seq=seq),
        grid=(nseq, heads),
        in_specs=[zspec(cb_q), zspec(cb_ff), zspec(cb_fb), zspec(cb_v), zspec(cb_g),
                  pl.BlockSpec((6, LANES), lambda b, h: (0, h)),
                  pl.BlockSpec((1, LANES), lambda b, h: (0, h))],
        out_specs=pl.BlockSpec((seq, LANES), lambda b, h: (b, h)),
        out_shape=jax.ShapeDtypeStruct((nseq * seq, heads * LANES), BF16),
        scratch_shapes=[pltpu.VMEM((seq, LANES), F32), pltpu.VMEM((seq, LANES), F32)]
        + 2 * [pltpu.VMEM((LANES, LANES), F32), pltpu.VMEM((CHUNK, LANES), F32), pltpu.VMEM((CHUNK, LANES), F32),
               pltpu.VMEM((CHUNK * SUBLANES, LANES), F32)],
        compiler_params=_params("parallel", "parallel"),
        name="hgrn",
    )(z, z, z, z, z, lbp, gn)


def _rope128(x, ct, st):
    rot = pltpu.roll(x, 32, 1) - pltpu.roll(x, 96, 1)
    return x * ct + rot * st


def _mla_kernel(q_ref, kn_ref, v_ref, kr_ref, ctq_ref, stq_ref, ctk_ref, stk_ref, gn_ref, out_ref,
                kcat_ref, vcat_ref, *, seq, tk, n_sub, scale):
    qi = pl.program_id(2)
    rt = _tile(seq, 512, SUBLANES)

    @pl.when(qi == 0)
    def _():
        def build(i, carry):
            rows = pl.ds(pl.multiple_of(i * rt, rt), rt)
            krr = _rope128(kr_ref[rows, :], ctk_ref[rows, :], stk_ref[rows, :])
            kcat_ref[rows, 0:LANES] = kn_ref[rows, :]
            kcat_ref[rows, LANES:2 * LANES] = krr.astype(BF16)
            vcat_ref[rows, 0:LANES] = v_ref[rows, :]
            vcat_ref[rows, LANES:2 * LANES] = jnp.ones((rt, LANES), BF16)
            return carry

        lax.fori_loop(0, seq // rt, build, 0)

    ts = q_ref.shape[0] // n_sub
    qcs = []
    for a in range(n_sub):
        sl = slice(a * ts, (a + 1) * ts)
        q = q_ref[sl, :].astype(F32)
        qrr = _rope128(q[:, LANES:], ctq_ref[sl, :], stq_ref[sl, :])
        qcs.append((jnp.concatenate([q[:, :LANES], qrr], axis=1) * (scale * LOG2E)).astype(BF16))

    def scores(kc):
        kblk = kcat_ref[pl.ds(pl.multiple_of(kc * tk, tk), tk), :]
        return tuple(lax.dot_general(qc, kblk, NT_DIMS, preferred_element_type=F32) for qc in qcs)

    def update(kc, s_all, state):
        vblk = vcat_ref[pl.ds(pl.multiple_of(kc * tk, tk), tk), :]
        out = []
        for (m, acc), s in zip(state, s_all):
            m_new = jnp.maximum(m, jnp.max(s, axis=-1, keepdims=True))
            p = jnp.exp2(s - m_new)
            acc = jnp.exp2(m - m_new) * acc + jnp.dot(p.astype(BF16), vblk, preferred_element_type=F32)
            out.append((m_new, acc))
        return tuple(out)

    def body(kc, carry):
        s_all, state = carry
        return scores(kc + 1), update(kc, s_all, state)

    nk = seq // tk
    init = tuple((jnp.full((ts, 1), -jnp.inf, F32), jnp.zeros((ts, 2 * LANES), F32)) for _ in range(n_sub))
    s_last, state = lax.fori_loop(0, nk - 1, body, (scores(0), init))
    res = update(nk - 1, s_last, state)
    for a in range(n_sub):
        acc = res[a][1]
        o = acc[:, :LANES] / acc[:, LANES:]
        out_ref[a * ts:(a + 1) * ts, :] = _rms(o, gn_ref[...]).astype(out_ref.dtype)


def _mla(q_raw, kv_raw, z, ct, st, gn, *, nseq, seq, heads, cb_kr, tq=512, tk=512, n_sub=2):
    tq = _tile(seq, tq, SUBLANES)
    tk = _tile(seq, tk, SUBLANES)
    nq = seq // tq
    scale = float((B_NOPE + B_ROPE) ** -0.5)
    return pl.pallas_call(
        functools.partial(_mla_kernel, seq=seq, tk=tk, n_sub=n_sub, scale=scale),
        grid=(nseq, heads, nq),
        in_specs=[pl.BlockSpec((tq, 2 * LANES), lambda b, h, i: (b * nq + i, h)),
                  pl.BlockSpec((seq, LANES), lambda b, h, i: (b, h)),
                  pl.BlockSpec((seq, LANES), lambda b, h, i: (b, heads + h)),
                  pl.BlockSpec((seq, LANES), lambda b, h, i: (b, cb_kr)),
                  pl.BlockSpec((tq, LANES), lambda b, h, i: (i, 0)),
                  pl.BlockSpec((tq, LANES), lambda b, h, i: (i, 0)),
                  pl.BlockSpec((seq, LANES), lambda b, h, i: (0, 0)),
                  pl.BlockSpec((seq, LANES), lambda b, h, i: (0, 0)),
                  pl.BlockSpec((1, LANES), lambda b, h, i: (0, h))],
        out_specs=pl.BlockSpec((tq, LANES), lambda b, h, i: (b * nq + i, h)),
        out_shape=jax.ShapeDtypeStruct((nseq * seq, heads * B_V), BF16),
        scratch_shapes=[pltpu.VMEM((seq, 2 * LANES), BF16), pltpu.VMEM((seq, 2 * LANES), BF16)],
        compiler_params=_params("parallel", "parallel", "arbitrary"),
        name="mla_attn",
    )(q_raw, kv_raw, kv_raw, z, ct, st, ct, st, gn)


def _conv_kernel(cb_ref, cc_ref, ch_ref, w_ref, bias_ref, gn_ref, out_ref, u_ref, *, seq, gdim):
    pad = SUBLANES
    zeros = jnp.zeros((pad, LANES), F32)
    u_ref[0:pad, :] = zeros
    u_ref[pad + seq:pad + seq + pad, :] = zeros
    rt = _tile(seq, 256, SUBLANES)

    def mk(i, carry):
        rows = pl.ds(pl.multiple_of(i * rt, rt), rt)
        u_ref[pl.ds(pl.multiple_of(i * rt, rt) + pad, rt), :] = cc_ref[rows, :] * ch_ref[rows, :]
        return carry

    lax.fori_loop(0, seq // rt, mk, 0)

    gi = lax.broadcasted_iota(jnp.int32, (LANES, LANES), 0) // gdim
    gj = lax.broadcasted_iota(jnp.int32, (LANES, LANES), 1) // gdim
    gmat = jnp.where(gi == gj, 1.0, 0.0).astype(BF16)
    ridx = lax.broadcasted_iota(jnp.int32, (rt, LANES), 0)
    w0, w1, w2 = w_ref[0:1, :], w_ref[1:2, :], w_ref[2:3, :]

    def body(i, carry):
        r0 = pl.multiple_of(i * rt, rt)
        cur = u_ref[pl.ds(r0 + pad, rt), :]
        before = u_ref[pl.ds(r0, pad), :][pad - 1:pad, :]
        after = u_ref[pl.ds(r0 + pad + rt, pad), :][0:1, :]
        prev = jnp.where(ridx == 0, before, pltpu.roll(cur, 1, 0))
        nxt = jnp.where(ridx == rt - 1, after, pltpu.roll(cur, rt - 1, 0))
        y = prev * w0 + cur * w1 + nxt * w2 + bias_ref[...]
        o = cb_ref[pl.ds(r0, rt), :] * y
        sq = o * o
        hi = sq.astype(BF16)
        lo = (sq - hi.astype(F32)).astype(BF16)
        ssum = jnp.dot(hi, gmat, preferred_element_type=F32) + jnp.dot(lo, gmat, preferred_element_type=F32)
        out_ref[pl.ds(r0, rt), :] = (o * lax.rsqrt(ssum * (1.0 / gdim) + RMS_EPS) * gn_ref[...]).astype(out_ref.dtype)
        return carry

    lax.fori_loop(0, seq // rt, body, 0)


def _conv(z, conv_w, conv_b, gn, *, nseq, seq, d_c, cb_b, cb_c, cb_h):
    ncb = d_c // LANES

    def zspec(cb):
        return pl.BlockSpec((seq, LANES), lambda b, j, cb=cb: (b, cb + j))

    return pl.pallas_call(
        functools.partial(_conv_kernel, seq=seq, gdim=d_c // C_GROUPS),
        grid=(nseq, ncb),
        in_specs=[zspec(cb_b), zspec(cb_c), zspec(cb_h),
                  pl.BlockSpec((3, LANES), lambda b, j: (0, j)),
                  pl.BlockSpec((1, LANES), lambda b, j: (0, j)),
                  pl.BlockSpec((1, LANES), lambda b, j: (0, j))],
        out_specs=pl.BlockSpec((seq, LANES), lambda b, j: (b, j)),
        out_shape=jax.ShapeDtypeStruct((nseq * seq, d_c), BF16),
        scratch_shapes=[pltpu.VMEM((seq + 2 * SUBLANES, LANES), F32)],
        compiler_params=_params("parallel", "parallel"),
        name="short_conv",
    )(z, z, z, conv_w, conv_b, gn)


def _xattn_kernel(q_ref, k_ref, v_ref, out_ref, *, hd, scale):
    outs = []
    for h in range(X_HEADS):
        sl = slice(h * hd, (h + 1) * hd)
        s = lax.dot_general(q_ref[:, sl], k_ref[:, sl], NT_DIMS, preferred_element_type=F32) * scale
        m = jnp.max(s, axis=-1, keepdims=True)
        p = jnp.exp(s - m)
        p = p / jnp.sum(p, axis=-1, keepdims=True)
        outs.append(jnp.dot(p.astype(BF16), v_ref[:, sl], preferred_element_type=F32))
    out_ref[...] = jnp.concatenate(outs, axis=1).astype(out_ref.dtype)


def _xattn(qx, kvx, *, nseq, seq, n_mem, xw, tm=512):
    tm = _tile(seq, tm, SUBLANES)
    nt = seq // tm
    hd = xw // X_HEADS
    return pl.pallas_call(
        functools.partial(_xattn_kernel, hd=hd, scale=float(hd ** -0.5)),
        grid=(nseq, nt),
        in_specs=[pl.BlockSpec((tm, xw), lambda b, i: (b * nt + i, 0)),
                  pl.BlockSpec((n_mem, xw), lambda b, i: (b, 0)),
                  pl.BlockSpec((n_mem, xw), lambda b, i: (b, 1))],
        out_specs=pl.BlockSpec((tm, xw), lambda b, i: (b * nt + i, 0)),
        out_shape=jax.ShapeDtypeStruct((nseq * seq, xw), BF16),
        compiler_params=_params("parallel", "parallel"),
        name="xattn",
    )(qx, kvx, kvx)


def _router_kernel(x_ref, g_ref, wh_ref, wl_ref, bias_ref, eid_ref, ew_ref):
    xn = _rms(x_ref[...], g_ref[...])
    xh = xn.astype(BF16)
    xl = (xn - xh.astype(F32)).astype(BF16)
    logits = (jnp.dot(xh, wh_ref[...], preferred_element_type=F32)
              + jnp.dot(xl, wh_ref[...], preferred_element_type=F32)
              + jnp.dot(xh, wl_ref[...], preferred_element_type=F32)) + bias_ref[...]
    lane = lax.broadcasted_iota(jnp.int32, logits.shape, 1)
    ninf = -jnp.inf
    glog = jnp.where(lane < N_GROUPS, logits, ninf)
    gmax = jnp.max(glog, axis=-1, keepdims=True)
    gsum = jnp.sum(jnp.exp(glog - gmax), axis=-1, keepdims=True)
    g_val = 1.0 / gsum
    g_idx = jnp.min(jnp.where(glog == gmax, lane, ROUTER_LANES), axis=-1, keepdims=True)
    lo = N_GROUPS + g_idx * EXP_PER_GROUP
    elog = jnp.where(lane >= lo, jnp.where(lane < lo + EXP_PER_GROUP, logits, ninf), ninf)
    m1 = jnp.max(elog, axis=-1, keepdims=True)
    i1 = jnp.min(jnp.where(elog == m1, lane, ROUTER_LANES), axis=-1, keepdims=True)
    elog2 = jnp.where(lane == i1, ninf, elog)
    m2 = jnp.max(elog2, axis=-1, keepdims=True)
    i2 = jnp.min(jnp.where(elog2 == m2, lane, ROUTER_LANES), axis=-1, keepdims=True)
    e2 = jnp.exp(m2 - m1)
    w1 = g_val / (1.0 + e2)
    w2 = g_val * e2 / (1.0 + e2)
    eid_ref[...] = jnp.where(lane == 0, i1 - N_GROUPS, jnp.where(lane == 1, i2 - N_GROUPS, 0))
    ew_ref[...] = jnp.where(lane == 0, w1, jnp.where(lane == 1, w2, 0.0))


def _router(x, gain, wh, wl, bias, *, tm=256):
    m, d = x.shape
    tm = _tile(m, tm, SUBLANES)
    return pl.pallas_call(
        _router_kernel,
        grid=(m // tm,),
        in_specs=[pl.BlockSpec((tm, d), lambda i: (i, 0)),
                  pl.BlockSpec((1, d), lambda i: (0, 0)),
                  pl.BlockSpec((d, ROUTER_LANES), lambda i: (0, 0)),
                  pl.BlockSpec((d, ROUTER_LANES), lambda i: (0, 0)),
                  pl.BlockSpec((1, ROUTER_LANES), lambda i: (0, 0))],
        out_specs=[pl.BlockSpec((tm, ROUTER_LANES), lambda i: (i, 0)),
                   pl.BlockSpec((tm, ROUTER_LANES), lambda i: (i, 0))],
        out_shape=[jax.ShapeDtypeStruct((m, ROUTER_LANES), jnp.int32),
                   jax.ShapeDtypeStruct((m, ROUTER_LANES), F32)],
        compiler_params=_params("parallel"),
        name="router",
    )(x, gain, wh, wl, bias)


def _row_copy(src_hbm, src_row, dst, dst_row, sem):
    return pltpu.make_async_copy(src_hbm.at[pl.ds(src_row, 1)], dst.at[pl.ds(dst_row, 1)], sem)


def _gather_kernel(nu_ref, tok_ref, tokn_ref, x_hbm, g_ref, out_ref, buf, sems, *, rows):
    i = pl.program_id(0)
    n_live = nu_ref[0]
    slot = i % 2

    def fetch(idx_ref, sl):
        def start(r, carry):
            _row_copy(x_hbm, idx_ref[0, 0, r], buf.at[sl], r, sems.at[sl]).start()
            return carry

        lax.fori_loop(0, rows, start, 0, unroll=8)

    @pl.when(i == 0)
    def _():
        fetch(tok_ref, 0)

    @pl.when(i + 1 < n_live)
    def _():
        fetch(tokn_ref, 1 - slot)

    @pl.when(i < n_live)
    def _():
        def wait(r, carry):
            _row_copy(x_hbm, 0, buf.at[slot], r, sems.at[slot]).wait()
            return carry

        lax.fori_loop(0, rows, wait, 0, unroll=8)
        out_ref[...] = _rms(buf[slot], g_ref[...]).astype(out_ref.dtype)

    @pl.when(i >= n_live)
    def _():
        out_ref[...] = jnp.zeros_like(out_ref)


def _gather_rows(x, gain, row_tok, n_used, *, rows):
    n_rows = row_tok.shape[0]
    d = x.shape[1]
    nb = n_rows // rows
    tok3 = row_tok.reshape(nb, 1, rows)
    grid_spec = pltpu.PrefetchScalarGridSpec(
        num_scalar_prefetch=1,
        grid=(nb,),
        in_specs=[pl.BlockSpec((1, 1, rows), lambda i, nu: (i, 0, 0), memory_space=pltpu.SMEM),
                  pl.BlockSpec((1, 1, rows), lambda i, nu: (jnp.minimum(i + 1, nb - 1), 0, 0),
                               memory_space=pltpu.SMEM),
                  pl.BlockSpec(memory_space=pl.ANY),
                  pl.BlockSpec((1, d), lambda i, nu: (0, 0))],
        out_specs=pl.BlockSpec((rows, d), lambda i, nu: (i, 0)),
        scratch_shapes=[pltpu.VMEM((2, rows, d), F32), pltpu.SemaphoreType.DMA((2,))],
    )
    return pl.pallas_call(
        functools.partial(_gather_kernel, rows=rows),
        grid_spec=grid_spec,
        out_shape=jax.ShapeDtypeStruct((n_rows, d), BF16),
        compiler_params=_params("arbitrary"),
        name="moe_gather",
    )(n_used, tok3, tok3, x, gain)


def _moe_up_kernel(be_ref, nu_ref, xs_ref, wg_ref, wu_ref, h_ref, wgb_ref, wub_ref):
    i = pl.program_id(1)
    prev = be_ref[jnp.maximum(i - 1, 0)]

    @pl.when(jnp.logical_or(i == 0, be_ref[i] != prev))
    def _():
        wgb_ref[...] = wg_ref[...].astype(BF16)
        wub_ref[...] = wu_ref[...].astype(BF16)

    @pl.when(i < nu_ref[0])
    def _():
        xn = xs_ref[...]
        a = jnp.dot(xn, wgb_ref[...], preferred_element_type=F32)
        b = jnp.dot(xn, wub_ref[...], preferred_element_type=F32)
        h_ref[...] = (a * _sigmoid(a) * b).astype(h_ref.dtype)

    @pl.when(i >= nu_ref[0])
    def _():
        h_ref[...] = jnp.zeros_like(h_ref)


def _moe_up(xs, w_gate, w_up, layer, blk_exp, n_used, *, bm, tn=512):
    n_rows, d = xs.shape
    de = w_gate.shape[-1]
    tn = _tile(de, tn)
    nb = n_rows // bm

    def row_map(j, i, be, nu):
        return (jnp.minimum(i, nu[0] - 1), 0)

    def w_map(j, i, be, nu):
        return (layer, be[i], 0, j)

    grid_spec = pltpu.PrefetchScalarGridSpec(
        num_scalar_prefetch=2,
        grid=(de // tn, nb),
        in_specs=[pl.BlockSpec((bm, d), row_map),
                  pl.BlockSpec((None, None, d, tn), w_map),
                  pl.BlockSpec((None, None, d, tn), w_map)],
        out_specs=pl.BlockSpec((bm, tn), lambda j, i, be, nu: (i, j)),
        scratch_shapes=[pltpu.VMEM((d, tn), BF16), pltpu.VMEM((d, tn), BF16)],
    )
    return pl.pallas_call(
        _moe_up_kernel,
        grid_spec=grid_spec,
        out_shape=jax.ShapeDtypeStruct((n_rows, de), BF16),
        compiler_params=_params("arbitrary", "arbitrary"),
        name="moe_up",
    )(blk_exp, n_used, xs, w_gate, w_up)


def _moe_down_kernel(be_ref, nu_ref, h_ref, wd_ref, y_ref, wdb_ref):
    i = pl.program_id(1)
    prev = be_ref[jnp.maximum(i - 1, 0)]

    @pl.when(jnp.logical_or(i == 0, be_ref[i] != prev))
    def _():
        wdb_ref[...] = wd_ref[...].astype(BF16)

    @pl.when(i < nu_ref[0])
    def _():
        y_ref[...] = jnp.dot(h_ref[...], wdb_ref[...], preferred_element_type=F32)

    @pl.when(i >= nu_ref[0])
    def _():
        y_ref[...] = jnp.zeros_like(y_ref)


def _moe_down(hs, w_down, layer, blk_exp, n_used, *, bm, tn=2048):
    n_rows, de = hs.shape
    d = w_down.shape[-1]
    tn = _tile(d, tn)
    nb = n_rows // bm
    grid_spec = pltpu.PrefetchScalarGridSpec(
        num_scalar_prefetch=2,
        grid=(d // tn, nb),
        in_specs=[pl.BlockSpec((bm, de), lambda j, i, be, nu: (jnp.minimum(i, nu[0] - 1), 0)),
                  pl.BlockSpec((None, None, de, tn), lambda j, i, be, nu: (layer, be[i], 0, j))],
        out_specs=pl.BlockSpec((bm, tn), lambda j, i, be, nu: (i, j)),
        scratch_shapes=[pltpu.VMEM((de, tn), BF16)],
    )
    return pl.pallas_call(
        _moe_down_kernel,
        grid_spec=grid_spec,
        out_shape=jax.ShapeDtypeStruct((n_rows, d), F32),
        compiler_params=_params("arbitrary", "arbitrary"),
        name="moe_down",
    )(blk_exp, n_used, hs, w_down)


def _combine_kernel(pos_ref, posn_ref, x_ref, ew_ref, g_ref, ys_hbm, out_ref, ybuf, sems, *, tm, nblk, final):
    i = pl.program_id(0)
    slot = i % 2

    def fetch(idx_ref, sl):
        def start(r, carry):
            for k in range(TOP_K):
                _row_copy(ys_hbm, idx_ref[0, 0, TOP_K * r + k], ybuf.at[sl, k], r, sems.at[sl]).start()
            return carry

        lax.fori_loop(0, tm, start, 0, unroll=4)

    @pl.when(i == 0)
    def _():
        fetch(pos_ref, 0)

    @pl.when(i + 1 < nblk)
    def _():
        fetch(posn_ref, 1 - slot)

    def wait(r, carry):
        for k in range(TOP_K):
            _row_copy(ys_hbm, 0, ybuf.at[slot, k], r, sems.at[slot]).wait()
        return carry

    lax.fori_loop(0, tm, wait, 0, unroll=4)
    w = ew_ref[...]
    y = x_ref[...]
    for k in range(TOP_K):
        y = y + w[:, k:k + 1] * ybuf[slot, k]
    if final:
        y = _rms(y, g_ref[...])
    out_ref[...] = y


def _combine(x, ew, pos, ys, gain, *, final, blk0=0, nblk=None, tm=256):
    m, d = x.shape
    tm = _tile(m, tm, SUBLANES)
    nb = m // tm
    nblk = nb if nblk is None else nblk
    pos3 = pos.reshape(nb, 1, TOP_K * tm)
    return pl.pallas_call(
        functools.partial(_combine_kernel, tm=tm, nblk=nblk, final=final),
        grid=(nblk,),
        in_specs=[pl.BlockSpec((1, 1, TOP_K * tm), lambda i: (blk0 + i, 0, 0), memory_space=pltpu.SMEM),
                  pl.BlockSpec((1, 1, TOP_K * tm), lambda i: (jnp.minimum(blk0 + i + 1, nb - 1), 0, 0),
                               memory_space=pltpu.SMEM),
                  pl.BlockSpec((tm, d), lambda i: (blk0 + i, 0)),
                  pl.BlockSpec((tm, ROUTER_LANES), lambda i: (blk0 + i, 0)),
                  pl.BlockSpec((1, d), lambda i: (0, 0)),
                  pl.BlockSpec(memory_space=pl.ANY)],
        out_specs=pl.BlockSpec((tm, d), lambda i: (i, 0)),
        out_shape=jax.ShapeDtypeStruct((nblk * tm, d), F32),
        scratch_shapes=[pltpu.VMEM((2, TOP_K, tm, d), F32), pltpu.SemaphoreType.DMA((2,))],
        compiler_params=_params("arbitrary"),
        name="moe_combine",
    )(pos3, pos3, x, ew, gain, ys)


def _dispatch_plan(eid, bm):
    n_pairs = eid.shape[0] * TOP_K
    flat_e = eid.reshape(-1)
    order = jnp.argsort(flat_e, stable=True).astype(jnp.int32)
    se = flat_e[order]
    bounds = jnp.searchsorted(se, jnp.arange(N_EXPERTS + 1, dtype=jnp.int32), side="left").astype(jnp.int32)
    start, counts = bounds[:-1], bounds[1:] - bounds[:-1]
    pcounts = (counts + bm - 1) // bm * bm
    pend = jnp.cumsum(pcounts)
    pstart = pend - pcounts
    dest = (pstart[se] + jnp.arange(n_pairs, dtype=jnp.int32) - start[se]).astype(jnp.int32)
    _, pos = lax.sort((order, dest), num_keys=1)
    n_rows = n_pairs + N_EXPERTS * bm
    blk_exp = jnp.minimum(
        jnp.searchsorted(pend, jnp.arange(n_rows // bm, dtype=jnp.int32) * bm, side="right"), N_EXPERTS - 1
    ).astype(jnp.int32)
    row = jnp.arange(n_rows, dtype=jnp.int32)
    row_e = blk_exp[row // bm]
    local = row - pstart[row_e]
    src = jnp.clip(start[row_e] + local, 0, n_pairs - 1)
    row_tok = jnp.where(local < counts[row_e], order[src] // TOP_K, 0).astype(jnp.int32)
    n_used = (pend[-1:] // bm).astype(jnp.int32)
    return row_tok, pos.astype(jnp.int32), blk_exp, n_used


def kernel(x_prompt, x_sample, mem_prompt, mem_sample, g_mix, w_in, a_lb_fwd, a_lb_bwd, a_norm, b_g_cq, b_w_uq, b_g_ckv, b_w_ukv, b_norm, c_conv_w, c_conv_b, c_norm, w_out, g_xattn, g_mem, w_xq, w_xk, w_xv, w_xo, g_ffn, w_rg, b_rg, w_re, b_re, w_gate, w_up, w_down, g_final):
    depth, d, _ = w_in.shape
    seq = x_prompt.shape[1]
    assert x_sample.shape[1] == seq and seq % CHUNK == 0
    n_mem = mem_prompt.shape[1]
    nb_p, nb_s = x_prompt.shape[0], x_sample.shape[0]
    nseq = nb_p + nb_s
    d_ak = a_lb_fwd.shape[1]
    a_heads = d_ak // LANES
    d_a = a_heads * A_HEAD_DIM
    q_lora = b_w_uq.shape[1]
    kv_lora = b_w_ukv.shape[1]
    b_heads = b_w_uq.shape[2] // (B_NOPE + B_ROPE)
    d_b = b_heads * B_V
    d_c = c_conv_b.shape[1]
    xw = w_xq.shape[2]
    assert d_a + d_b + d_c == d

    x = jnp.concatenate([x_prompt.reshape(-1, d), x_sample.reshape(-1, d)], axis=0)
    mem = jnp.concatenate([mem_prompt.reshape(-1, d), mem_sample.reshape(-1, d)], axis=0)
    n_tok = x.shape[0]

    sizes = (d_ak, d_ak, d_ak, d_a, d_a, q_lora, kv_lora, B_ROPE, d_c, d_c, d_c)
    offs = [0]
    for s in sizes:
        offs.append(offs[-1] + s)
    kr_pad = LANES - B_ROPE
    zw_used = offs[-1] + kr_pad
    zw = _round_up(zw_used, MXU_DIM)
    w_in_z = jnp.concatenate(
        [w_in[:, :, :offs[8]], jnp.zeros((depth, d, kr_pad), w_in.dtype), w_in[:, :, offs[8]:],
         jnp.zeros((depth, d, zw - zw_used), w_in.dtype)], axis=2).astype(BF16)
    zoff = {"a_q": offs[0], "a_ff": offs[1], "a_fb": offs[2], "a_i": offs[3], "a_g": offs[4], "b_cq": offs[5],
            "b_ckv": offs[6], "b_kr": offs[7], "c_b": offs[8] + kr_pad, "c_c": offs[9] + kr_pad,
            "c_h": offs[10] + kr_pad}
    assert all(v % LANES == 0 for v in zoff.values())
    assert zoff["b_cq"] % q_lora == 0 and zoff["b_ckv"] % kv_lora == 0

    qk_pad = 2 * LANES - (B_NOPE + B_ROPE)
    w_uq_z = jnp.pad(b_w_uq.reshape(depth, q_lora, b_heads, B_NOPE + B_ROPE),
                     ((0, 0), (0, 0), (0, 0), (0, qk_pad))).reshape(depth, q_lora, b_heads * 2 * LANES).astype(BF16)
    w_ukv_z = b_w_ukv.reshape(depth, kv_lora, b_heads, 2, B_V).transpose(0, 1, 3, 2, 4).reshape(
        depth, kv_lora, 2 * d_b).astype(BF16)
    w_out_z = w_out.astype(BF16)
    w_xq_z = w_xq.astype(BF16)
    w_xkv_z = jnp.concatenate([w_xk, w_xv], axis=2).astype(BF16)
    w_xo_z = w_xo.astype(BF16)
    w_r = jnp.concatenate([w_rg, w_re, jnp.zeros((depth, d, ROUTER_LANES - N_GROUPS - N_EXPERTS), F32)], axis=2)
    w_r_hi = w_r.astype(BF16)
    w_r_lo = (w_r - w_r_hi.astype(F32)).astype(BF16)
    b_r = jnp.concatenate([b_rg, b_re, jnp.zeros((depth, ROUTER_LANES - N_GROUPS - N_EXPERTS), F32)], axis=1)

    def lower_bounds(p):
        lb = jnp.cumsum(jax.nn.softmax(p.astype(F32), axis=0), axis=0)
        lb = lb - lb[0]
        return jnp.stack([jnp.log(lb), jnp.log1p(-lb), 1.0 - lb], axis=1)

    lbp = jnp.concatenate([lower_bounds(a_lb_fwd), lower_bounds(a_lb_bwd)], axis=1)

    inv = 1.0 / (ROPE_THETA ** (jnp.arange(0, B_ROPE, 2, dtype=F32) / B_ROPE))
    ang = jnp.arange(seq, dtype=F32)[:, None] * inv[None, :]
    zpad = jnp.zeros((seq, LANES - B_ROPE), F32)
    rope_c = jnp.concatenate([jnp.cos(ang), jnp.cos(ang), zpad], axis=1)
    rope_s = jnp.concatenate([jnp.sin(ang), jnp.sin(ang), zpad], axis=1)

    bm = 256 if n_tok * TOP_K >= N_EXPERTS * 256 else 16

    for l in range(depth):
        z = _matmul([(x, d, 0, 0)], w_in_z[l], gain=g_mix[l], out_dtype=F32, tm=512, tn=768, name="in_proj")
        o_a = _hgrn(z, lbp[l], a_norm[l].reshape(1, d_a), nseq=nseq, seq=seq, heads=a_heads,
                    cb_q=zoff["a_q"] // LANES, cb_ff=zoff["a_ff"] // LANES, cb_fb=zoff["a_fb"] // LANES,
                    cb_v=zoff["a_i"] // LANES, cb_g=zoff["a_g"] // LANES)
        q_raw = _matmul([(z, q_lora, zoff["b_cq"] // q_lora, 0)], w_uq_z[l], gain=b_g_cq[l], out_dtype=BF16,
                        tm=1024, tn=1024, name="q_up")
        kv_raw = _matmul([(z, kv_lora, zoff["b_ckv"] // kv_lora, 0)], w_ukv_z[l], gain=b_g_ckv[l],
                         out_dtype=BF16, tm=1024, tn=1024, name="kv_up")
        o_b = _mla(q_raw, kv_raw, z, rope_c, rope_s, b_norm[l].reshape(1, d_b), nseq=nseq, seq=seq,
                   heads=b_heads, cb_kr=zoff["b_kr"] // LANES)
        o_c = _conv(z, c_conv_w[l], c_conv_b[l].reshape(1, d_c), c_norm[l].reshape(1, d_c), nseq=nseq, seq=seq,
                    d_c=d_c, cb_b=zoff["c_b"] // LANES, cb_c=zoff["c_c"] // LANES, cb_h=zoff["c_h"] // LANES)
        kq = d_a
        assert d_b % kq == 0 and d_c == kq
        mix = [(o_a, kq, 0, 0)] + [(o_b, kq, i, 1 + i) for i in range(d_b // kq)] + [(o_c, kq, 0, 1 + d_b // kq)]
        x = _matmul(mix, w_out_z[l], res=x, out_dtype=F32, tm=512, tn=1024, name="out_proj")

        qx = _matmul([(x, d, 0, 0)], w_xq_z[l], gain=g_xattn[l], out_dtype=BF16, tm=512, tn=1024, name="xq_proj")
        kvx = _matmul([(mem, d, 0, 0)], w_xkv_z[l], gain=g_mem[l], out_dtype=BF16, tm=512, tn=1024, name="xkv_proj")
        ox = _xattn(qx, kvx, nseq=nseq, seq=seq, n_mem=n_mem, xw=xw)
        x = _matmul([(ox, xw, 0, 0)], w_xo_z[l], res=x, out_dtype=F32, tm=512, tn=1024, name="xo_proj")

        gain_f = g_ffn[l].reshape(1, d).astype(F32)
        eid, ew = _router(x, gain_f, w_r_hi[l], w_r_lo[l], b_r[l].reshape(1, ROUTER_LANES))
        row_tok, pos, blk_exp, n_used = _dispatch_plan(eid[:, :TOP_K], bm)
        xs = _gather_rows(x, gain_f, row_tok, n_used, rows=bm)
        hs = _moe_up(xs, w_gate, w_up, l, blk_exp, n_used, bm=bm)
        ys = _moe_down(hs, w_down, l, blk_exp, n_used, bm=bm)
        gain_o = g_final.reshape(1, d).astype(F32)
        if l < depth - 1:
            x = _combine(x, ew, pos, ys, gain_o, final=False)

    tmc = _tile(n_tok, 256, SUBLANES)
    assert (nb_p * seq) % tmc == 0
    nblk_p = nb_p * seq // tmc
    y_p = _combine(x, ew, pos, ys, gain_o, final=True, blk0=0, nblk=nblk_p)
    y_s = _combine(x, ew, pos, ys, gain_o, final=True, blk0=nblk_p, nblk=n_tok // tmc - nblk_p)
    return (y_p.reshape(nb_p, seq, d), y_s.reshape(nb_s, seq, d))
```

```python
import functools

import jax
import jax.numpy as jnp
from jax import lax
from jax.experimental import pallas as pl
from jax.experimental.pallas import tpu as pltpu

F32 = jnp.float32
BF16 = jnp.bfloat16

RMS_EPS = 1e-6
LANES = 128
SUBLANES = 8
MXU_DIM = 256
VMEM_LIMIT = 56 * 1024 * 1024

A_HEAD_DIM = 128
CHUNK = 64
B_V = 128
B_NOPE = 128
B_ROPE = 64
ROPE_THETA = 10000.0
C_GROUPS = 16
X_HEADS = 4
N_GROUPS = 8
EXP_PER_GROUP = 8
N_EXPERTS = N_GROUPS * EXP_PER_GROUP
TOP_K = 2
ROUTER_LANES = 128
LOG2E = 1.4426950408889634

NT_DIMS = (((1,), (1,)), ((), ()))
TN_DIMS = (((0,), (0,)), ((), ()))


def _tile(n, pref, align=LANES):
    if n <= pref:
        return n
    t = (pref // align) * align
    while t >= align:
        if n % t == 0:
            return t
        t -= align
    return n


def _round_up(n, m):
    return (n + m - 1) // m * m


def _params(*sem):
    return pltpu.CompilerParams(dimension_semantics=sem, vmem_limit_bytes=VMEM_LIMIT)


def _sigmoid(x):
    return 1.0 / (1.0 + jnp.exp(-x))


def _rms(x, gain):
    ms = jnp.mean(x * x, axis=-1, keepdims=True)
    return x * lax.rsqrt(ms + RMS_EPS) * gain


def _mm_kernel(*refs, n_lhs, normed, has_res):
    refs = list(refs)
    gain_ref = refs.pop(0) if normed else None
    lhs = refs[:n_lhs]
    ws = refs[n_lhs:2 * n_lhs]
    rest = refs[2 * n_lhs:]
    res_ref = rest.pop(0) if has_res else None
    out_ref = rest.pop(0)
    if normed:
        xn_ref = rest.pop(0)

        @pl.when(pl.program_id(1) == 0)
        def _():
            xn_ref[...] = _rms(lhs[0][...].astype(F32), gain_ref[...]).astype(BF16)

        acc = jnp.dot(xn_ref[...], ws[0][...], preferred_element_type=F32)
    else:
        acc = None
        for l_ref, w_ref in zip(lhs, ws):
            d = jnp.dot(l_ref[...].astype(BF16), w_ref[...], preferred_element_type=F32)
            acc = d if acc is None else acc + d
    if has_res:
        acc = acc + res_ref[...]
    out_ref[...] = acc.astype(out_ref.dtype)


def _matmul(lhs_list, w, *, gain=None, res=None, out_dtype=F32, tm=512, tn=512, name="mm"):
    m = lhs_list[0][0].shape[0]
    n = w.shape[1]
    tm = _tile(m, tm, SUBLANES)
    tn = _tile(n, tn)
    normed = gain is not None
    assert not normed or len(lhs_list) == 1
    in_specs, args = [], []
    if normed:
        k0 = lhs_list[0][1]
        in_specs.append(pl.BlockSpec((1, k0), lambda i, j: (0, 0)))
        args.append(gain.reshape(1, k0).astype(F32))
    for arr, k, cb, _ in lhs_list:
        in_specs.append(pl.BlockSpec((tm, k), lambda i, j, cb=cb: (i, cb)))
        args.append(arr)
    for _, k, _, rb in lhs_list:
        in_specs.append(pl.BlockSpec((k, tn), lambda i, j, rb=rb: (rb, j)))
        args.append(w)
    if res is not None:
        in_specs.append(pl.BlockSpec((tm, tn), lambda i, j: (i, j)))
        args.append(res)
    scratch = [pltpu.VMEM((tm, lhs_list[0][1]), BF16)] if normed else []
    return pl.pallas_call(
        functools.partial(_mm_kernel, n_lhs=len(lhs_list), normed=normed, has_res=res is not None),
        grid=(m // tm, n // tn),
        in_specs=in_specs,
        out_specs=pl.BlockSpec((tm, tn), lambda i, j: (i, j)),
        out_shape=jax.ShapeDtypeStruct((m, n), out_dtype),
        scratch_shapes=scratch,
        compiler_params=_params("parallel", "arbitrary"),
        name=name,
    )(*args)


def _gla_gates(dirn, r0, q_ref, f_ref, lbp_ref, tri):
    c = CHUNK
    rows = pl.ds(r0, c)
    x = f_ref[rows, :]
    e = jnp.exp(-jnp.abs(x))
    inv = 1.0 / (1.0 + e)
    log_sig = jnp.minimum(x, 0.0) - jnp.log1p(e)
    sig_neg = jnp.where(x >= 0, e * inv, inv)
    la = lbp_ref[3 * dirn:3 * dirn + 1, :]
    lc = lbp_ref[3 * dirn + 1:3 * dirn + 2, :] + log_sig
    logf = jnp.maximum(la, lc) + jnp.log1p(jnp.exp(-jnp.abs(la - lc)))
    kk = lbp_ref[3 * dirn + 2:3 * dirn + 3, :] * sig_neg

    hi = logf.astype(BF16)
    r1 = logf - hi.astype(F32)
    mid = r1.astype(BF16)
    lo = (r1 - mid.astype(F32)).astype(BF16)
    b = (jnp.dot(tri, hi, preferred_element_type=F32) + jnp.dot(tri, mid, preferred_element_type=F32)
         + jnp.dot(tri, lo, preferred_element_type=F32))

    qx = q_ref[rows, :]
    return qx * _sigmoid(qx), kk, b


def _gla_scores(dirn, gates, bsc, ksc, msc, ones):
    c = CHUNK
    fwd = dirn == 0
    qs, kk, b = gates
    b_end = b[c - 1:c, :] if fwd else b[0:1, :]
    qt = (qs * jnp.exp(b)).astype(BF16)
    kt = (kk * jnp.exp(b_end - b)).astype(BF16)
    dec = jnp.exp(b_end)

    bsc[...] = b
    ksc[...] = kk
    nblk = c // SUBLANES
    zero = jnp.zeros((SUBLANES, LANES), F32)

    def blk(a, i):
        return a[SUBLANES * i:SUBLANES * (i + 1)]

    def brow(r):
        return bsc[pl.ds(r, 1), :]

    q_anchor = [brow(SUBLANES * i if fwd else SUBLANES * i + SUBLANES - 1) for i in range(nblk)]
    k_anchor = [brow(SUBLANES * j + SUBLANES - 1 if fwd else SUBLANES * j) for j in range(nblk)]
    qh = [blk(qs, i) * jnp.exp(blk(b, i) - q_anchor[i]) for i in range(nblk)]
    kh = [blk(kk, j) * jnp.exp(k_anchor[j] - blk(b, j)) for j in range(nblk)]
    lhs_cols, rhs_cols = [], []
    for j in (range(nblk - 1) if fwd else range(1, nblk)):
        col = []
        for i in range(nblk):
            past = (j < i) if fwd else (j > i)
            col.append(qh[i] * jnp.exp(q_anchor[i] - k_anchor[j]) if past else zero)
        lhs_cols.append(jnp.concatenate(col, axis=0))
        rhs_cols.append(jnp.concatenate([kh[j] if r == j else zero for r in range(nblk)], axis=0))
    att = lax.dot_general(jnp.concatenate(lhs_cols, axis=1).astype(BF16), jnp.concatenate(rhs_cols, axis=1).astype(BF16),
                          NT_DIMS, preferred_element_type=F32)

    for i in range(nblk):
        for u in range(SUBLANES):
            s = SUBLANES * i + u
            prod = (blk(qs, i) * ksc[pl.ds(s, 1), :]) * jnp.exp(jnp.minimum(blk(b, i) - brow(s), 0.0))
            msc[SUBLANES * s:SUBLANES * (s + 1), :] = prod
    rs = jnp.dot(msc[...].astype(BF16), ones, preferred_element_type=F32)
    lane = lax.broadcasted_iota(jnp.int32, (SUBLANES, LANES), 1)
    sub = lax.broadcasted_iota(jnp.int32, (SUBLANES, LANES), 0)
    diag = []
    for i in range(nblk):
        dblk = zero
        for u in range(SUBLANES):
            s = SUBLANES * i + u
            dblk = jnp.where(lane == s, rs[SUBLANES * s:SUBLANES * (s + 1)], dblk)
        t = sub + SUBLANES * i
        diag.append(jnp.where((lane <= t) if fwd else (lane >= t), dblk, 0.0))
    att = att + jnp.concatenate(diag, axis=0)[:, :c]
    return att.astype(BF16), qt, kt, dec


def _gla_state(r0, scores, v_ref, st_ref, o_ref):
    att, qt, kt, dec = scores
    rows = pl.ds(r0, CHUNK)
    vb = v_ref[rows, :].astype(BF16)
    st = st_ref[...]
    o_ref[rows, :] = (lax.dot_general(qt, st.astype(BF16), NT_DIMS, preferred_element_type=F32)
                      + jnp.dot(att, vb, preferred_element_type=F32))
    st_ref[...] = st * dec + lax.dot_general(vb, kt, TN_DIMS, preferred_element_type=F32)


def _hgrn_kernel(q_ref, ff_ref, fb_ref, v_ref, g_ref, lbp_ref, gn_ref, out_ref,
                 of_ref, ob_ref, stf, bf, kf, mf, stb, bb, kb, mb, *, seq):
    c = CHUNK
    n = seq // c
    stf[...] = jnp.zeros_like(stf)
    stb[...] = jnp.zeros_like(stb)
    row = lax.broadcasted_iota(jnp.int32, (c, c), 0)
    col = lax.broadcasted_iota(jnp.int32, (c, c), 1)
    tri_f = jnp.where(col <= row, 1.0, 0.0).astype(BF16)
    tri_b = jnp.where(col >= row, 1.0, 0.0).astype(BF16)
    ones = jnp.ones((LANES, LANES), BF16)

    def gates(i):
        return (_gla_gates(0, pl.multiple_of(i * c, c), q_ref, ff_ref, lbp_ref, tri_f),
                _gla_gates(1, pl.multiple_of((n - 1 - i) * c, c), q_ref, fb_ref, lbp_ref, tri_b))

    def scores(g):
        return (_gla_scores(0, g[0], bf, kf, mf, ones), _gla_scores(1, g[1], bb, kb, mb, ones))

    def body(i, carry):
        g_next, s_cur = carry
        g_new = gates(jnp.minimum(i + 2, n - 1))
        s_next = scores(g_next)
        _gla_state(pl.multiple_of(i * c, c), s_cur[0], v_ref, stf, of_ref)
        _gla_state(pl.multiple_of((n - 1 - i) * c, c), s_cur[1], v_ref, stb, ob_ref)
        return g_new, s_next

    lax.fori_loop(0, n, body, (gates(jnp.minimum(1, n - 1)), scores(gates(0))))

    rt = _tile(seq, 256, SUBLANES)

    def fin(i, carry):
        rows = pl.ds(pl.multiple_of(i * rt, rt), rt)
        o = of_ref[rows, :] + ob_ref[rows, :]
        g = g_ref[rows, :]
        out_ref[rows, :] = (_rms(o, gn_ref[...]) * (g * _sigmoid(g))).astype(out_ref.dtype)
        return carry

    lax.fori_loop(0, seq // rt, fin, 0)


def _hgrn(z, lbp, gn, *, nseq, seq, heads, cb_q, cb_ff, cb_fb, cb_v, cb_g):
    def zspec(cb):
        return pl.BlockSpec((seq, LANES), lambda b, h, cb=cb: (b, cb + h))

    return pl.pallas_call(
        functools.partial(_hgrn_kernel, seq=seq),
        grid=(nseq, heads),
        in_specs=[zspec(cb_q), zspec(cb_ff), zspec(cb_fb), zspec(cb_v), zspec(cb_g),
                  pl.BlockSpec((6, LANES), lambda b, h: (0, h)),
                  pl.BlockSpec((1, LANES), lambda b, h: (0, h))],
        out_specs=pl.BlockSpec((seq, LANES), lambda b, h: (b, h)),
        out_shape=jax.ShapeDtypeStruct((nseq * seq, heads * LANES), BF16),
        scratch_shapes=[pltpu.VMEM((seq, LANES), F32), pltpu.VMEM((seq, LANES), F32)]
        + 2 * [pltpu.VMEM((LANES, LANES), F32), pltpu.VMEM((CHUNK, LANES), F32), pltpu.VMEM((CHUNK, LANES), F32),
               pltpu.VMEM((CHUNK * SUBLANES, LANES), F32)],
        compiler_params=_params("parallel", "parallel"),
        name="hgrn",
    )(z, z, z, z, z, lbp, gn)


def _rope128(x, ct, st):
    rot = pltpu.roll(x, 32, 1) - pltpu.roll(x, 96, 1)
    return x * ct + rot * st


def _mla_kernel(q_ref, kn_ref, v_ref, kr_ref, ctq_ref, stq_ref, ctk_ref, stk_ref, gn_ref, out_ref,
                kcat_ref, vcat_ref, *, seq, tk, n_sub, scale):
    qi = pl.program_id(2)
    rt = _tile(seq, 512, SUBLANES)

    @pl.when(qi == 0)
    def _():
        def build(i, carry):
            rows = pl.ds(pl.multiple_of(i * rt, rt), rt)
            krr = _rope128(kr_ref[rows, :], ctk_ref[rows, :], stk_ref[rows, :])
            kcat_ref[rows, 0:LANES] = kn_ref[rows, :]
            kcat_ref[rows, LANES:2 * LANES] = krr.astype(BF16)
            vcat_ref[rows, 0:LANES] = v_ref[rows, :]
            vcat_ref[rows, LANES:2 * LANES] = jnp.ones((rt, LANES), BF16)
            return carry

        lax.fori_loop(0, seq // rt, build, 0)

    ts = q_ref.shape[0] // n_sub
    qcs = []
    for a in range(n_sub):
        sl = slice(a * ts, (a + 1) * ts)
        q = q_ref[sl, :].astype(F32)
        qrr = _rope128(q[:, LANES:], ctq_ref[sl, :], stq_ref[sl, :])
        qcs.append((jnp.concatenate([q[:, :LANES], qrr], axis=1) * (scale * LOG2E)).astype(BF16))

    def scores(kc):
        kblk = kcat_ref[pl.ds(pl.multiple_of(kc * tk, tk), tk), :]
        return tuple(lax.dot_general(qc, kblk, NT_DIMS, preferred_element_type=F32) for qc in qcs)

    def update(kc, s_all, state):
        vblk = vcat_ref[pl.ds(pl.multiple_of(kc * tk, tk), tk), :]
        out = []
        for (m, acc), s in zip(state, s_all):
            m_new = jnp.maximum(m, jnp.max(s, axis=-1, keepdims=True))
            p = jnp.exp2(s - m_new)
            acc = jnp.exp2(m - m_new) * acc + jnp.dot(p.astype(BF16), vblk, preferred_element_type=F32)
            out.append((m_new, acc))
        return tuple(out)

    def body(kc, carry):
        s_all, state = carry
        return scores(kc + 1), update(kc, s_all, state)

    nk = seq // tk
    init = tuple((jnp.full((ts, 1), -jnp.inf, F32), jnp.zeros((ts, 2 * LANES), F32)) for _ in range(n_sub))
    s_last, state = lax.fori_loop(0, nk - 1, body, (scores(0), init))
    res = update(nk - 1, s_last, state)
    for a in range(n_sub):
        acc = res[a][1]
        o = acc[:, :LANES] / acc[:, LANES:]
        out_ref[a * ts:(a + 1) * ts, :] = _rms(o, gn_ref[...]).astype(out_ref.dtype)


def _mla(q_raw, kv_raw, z, ct, st, gn, *, nseq, seq, heads, cb_kr, tq=1024, tk=4096, n_sub=2):
    tq = _tile(seq, tq, SUBLANES)
    tk = _tile(seq, tk, SUBLANES)
    nq = seq // tq
    scale = float((B_NOPE + B_ROPE) ** -0.5)
    return pl.pallas_call(
        functools.partial(_mla_kernel, seq=seq, tk=tk, n_sub=n_sub, scale=scale),
        grid=(nseq, heads, nq),
        in_specs=[pl.BlockSpec((tq, 2 * LANES), lambda b, h, i: (b * nq + i, h)),
                  pl.BlockSpec((seq, LANES), lambda b, h, i: (b, h)),
                  pl.BlockSpec((seq, LANES), lambda b, h, i: (b, heads + h)),
                  pl.BlockSpec((seq, LANES), lambda b, h, i: (b, cb_kr)),
                  pl.BlockSpec((tq, LANES), lambda b, h, i: (i, 0)),
                  pl.BlockSpec((tq, LANES), lambda b, h, i: (i, 0)),
                  pl.BlockSpec((seq, LANES), lambda b, h, i: (0, 0)),
                  pl.BlockSpec((seq, LANES), lambda b, h, i: (0, 0)),
                  pl.BlockSpec((1, LANES), lambda b, h, i: (0, h))],
        out_specs=pl.BlockSpec((tq, LANES), lambda b, h, i: (b * nq + i, h)),
        out_shape=jax.ShapeDtypeStruct((nseq * seq, heads * B_V), BF16),
        scratch_shapes=[pltpu.VMEM((seq, 2 * LANES), BF16), pltpu.VMEM((seq, 2 * LANES), BF16)],
        compiler_params=_params("parallel", "parallel", "arbitrary"),
        name="mla_attn",
    )(q_raw, kv_raw, kv_raw, z, ct, st, ct, st, gn)


def _conv_kernel(cb_ref, cc_ref, ch_ref, w_ref, bias_ref, gn_ref, out_ref, u_ref, *, seq, gdim):
    pad = SUBLANES
    zeros = jnp.zeros((pad, LANES), F32)
    u_ref[0:pad, :] = zeros
    u_ref[pad + seq:pad + seq + pad, :] = zeros
    rt = _tile(seq, 256, SUBLANES)

    def mk(i, carry):
        rows = pl.ds(pl.multiple_of(i * rt, rt), rt)
        u_ref[pl.ds(pl.multiple_of(i * rt, rt) + pad, rt), :] = cc_ref[rows, :] * ch_ref[rows, :]
        return carry

    lax.fori_loop(0, seq // rt, mk, 0)

    gi = lax.broadcasted_iota(jnp.int32, (LANES, LANES), 0) // gdim
    gj = lax.broadcasted_iota(jnp.int32, (LANES, LANES), 1) // gdim
    gmat = jnp.where(gi == gj, 1.0, 0.0).astype(BF16)
    ridx = lax.broadcasted_iota(jnp.int32, (rt, LANES), 0)
    w0, w1, w2 = w_ref[0:1, :], w_ref[1:2, :], w_ref[2:3, :]

    def body(i, carry):
        r0 = pl.multiple_of(i * rt, rt)
        cur = u_ref[pl.ds(r0 + pad, rt), :]
        before = u_ref[pl.ds(r0, pad), :][pad - 1:pad, :]
        after = u_ref[pl.ds(r0 + pad + rt, pad), :][0:1, :]
        prev = jnp.where(ridx == 0, before, pltpu.roll(cur, 1, 0))
        nxt = jnp.where(ridx == rt - 1, after, pltpu.roll(cur, rt - 1, 0))
        y = prev * w0 + cur * w1 + nxt * w2 + bias_ref[...]
        o = cb_ref[pl.ds(r0, rt), :] * y
        sq = o * o
        hi = sq.astype(BF16)
        lo = (sq - hi.astype(F32)).astype(BF16)
        ssum = jnp.dot(hi, gmat, preferred_element_type=F32) + jnp.dot(lo, gmat, preferred_element_type=F32)
        out_ref[pl.ds(r0, rt), :] = (o * lax.rsqrt(ssum * (1.0 / gdim) + RMS_EPS) * gn_ref[...]).astype(out_ref.dtype)
        return carry

    lax.fori_loop(0, seq // rt, body, 0)


def _conv(z, conv_w, conv_b, gn, *, nseq, seq, d_c, cb_b, cb_c, cb_h):
    ncb = d_c // LANES

    def zspec(cb):
        return pl.BlockSpec((seq, LANES), lambda b, j, cb=cb: (b, cb + j))

    return pl.pallas_call(
        functools.partial(_conv_kernel, seq=seq, gdim=d_c // C_GROUPS),
        grid=(nseq, ncb),
        in_specs=[zspec(cb_b), zspec(cb_c), zspec(cb_h),
                  pl.BlockSpec((3, LANES), lambda b, j: (0, j)),
                  pl.BlockSpec((1, LANES), lambda b, j: (0, j)),
                  pl.BlockSpec((1, LANES), lambda b, j: (0, j))],
        out_specs=pl.BlockSpec((seq, LANES), lambda b, j: (b, j)),
        out_shape=jax.ShapeDtypeStruct((nseq * seq, d_c), BF16),
        scratch_shapes=[pltpu.VMEM((seq + 2 * SUBLANES, LANES), F32)],
        compiler_params=_params("parallel", "parallel"),
        name="short_conv",
    )(z, z, z, conv_w, conv_b, gn)


def _xattn_kernel(q_ref, k_ref, v_ref, out_ref, *, hd, scale):
    outs = []
    for h in range(X_HEADS):
        sl = slice(h * hd, (h + 1) * hd)
        s = lax.dot_general(q_ref[:, sl], k_ref[:, sl], NT_DIMS, preferred_element_type=F32) * scale
        m = jnp.max(s, axis=-1, keepdims=True)
        p = jnp.exp(s - m)
        p = p / jnp.sum(p, axis=-1, keepdims=True)
        outs.append(jnp.dot(p.astype(BF16), v_ref[:, sl], preferred_element_type=F32))
    out_ref[...] = jnp.concatenate(outs, axis=1).astype(out_ref.dtype)


def _xattn(qx, kvx, *, nseq, seq, n_mem, xw, tm=512):
    tm = _tile(seq, tm, SUBLANES)
    nt = seq // tm
    hd = xw // X_HEADS
    return pl.pallas_call(
        functools.partial(_xattn_kernel, hd=hd, scale=float(hd ** -0.5)),
        grid=(nseq, nt),
        in_specs=[pl.BlockSpec((tm, xw), lambda b, i: (b * nt + i, 0)),
                  pl.BlockSpec((n_mem, xw), lambda b, i: (b, 0)),
                  pl.BlockSpec((n_mem, xw), lambda b, i: (b, 1))],
        out_specs=pl.BlockSpec((tm, xw), lambda b, i: (b * nt + i, 0)),
        out_shape=jax.ShapeDtypeStruct((nseq * seq, xw), BF16),
        compiler_params=_params("parallel", "parallel"),
        name="xattn",
    )(qx, kvx, kvx)


def _router_kernel(x_ref, g_ref, wh_ref, wl_ref, bias_ref, eid_ref, ew_ref):
    xn = _rms(x_ref[...], g_ref[...])
    xh = xn.astype(BF16)
    xl = (xn - xh.astype(F32)).astype(BF16)
    logits = (jnp.dot(xh, wh_ref[...], preferred_element_type=F32)
              + jnp.dot(xl, wh_ref[...], preferred_element_type=F32)
              + jnp.dot(xh, wl_ref[...], preferred_element_type=F32)) + bias_ref[...]
    lane = lax.broadcasted_iota(jnp.int32, logits.shape, 1)
    ninf = -jnp.inf
    glog = jnp.where(lane < N_GROUPS, logits, ninf)
    gmax = jnp.max(glog, axis=-1, keepdims=True)
    gsum = jnp.sum(jnp.exp(glog - gmax), axis=-1, keepdims=True)
    g_val = 1.0 / gsum
    g_idx = jnp.min(jnp.where(glog == gmax, lane, ROUTER_LANES), axis=-1, keepdims=True)
    lo = N_GROUPS + g_idx * EXP_PER_GROUP
    elog = jnp.where(lane >= lo, jnp.where(lane < lo + EXP_PER_GROUP, logits, ninf), ninf)
    m1 = jnp.max(elog, axis=-1, keepdims=True)
    i1 = jnp.min(jnp.where(elog == m1, lane, ROUTER_LANES), axis=-1, keepdims=True)
    elog2 = jnp.where(lane == i1, ninf, elog)
    m2 = jnp.max(elog2, axis=-1, keepdims=True)
    i2 = jnp.min(jnp.where(elog2 == m2, lane, ROUTER_LANES), axis=-1, keepdims=True)
    e2 = jnp.exp(m2 - m1)
    w1 = g_val / (1.0 + e2)
    w2 = g_val * e2 / (1.0 + e2)
    eid_ref[...] = jnp.where(lane == 0, i1 - N_GROUPS, jnp.where(lane == 1, i2 - N_GROUPS, 0))
    ew_ref[...] = jnp.where(lane == 0, w1, jnp.where(lane == 1, w2, 0.0))


def _router(x, gain, wh, wl, bias, *, tm=256):
    m, d = x.shape
    tm = _tile(m, tm, SUBLANES)
    return pl.pallas_call(
        _router_kernel,
        grid=(m // tm,),
        in_specs=[pl.BlockSpec((tm, d), lambda i: (i, 0)),
                  pl.BlockSpec((1, d), lambda i: (0, 0)),
                  pl.BlockSpec((d, ROUTER_LANES), lambda i: (0, 0)),
                  pl.BlockSpec((d, ROUTER_LANES), lambda i: (0, 0)),
                  pl.BlockSpec((1, ROUTER_LANES), lambda i: (0, 0))],
        out_specs=[pl.BlockSpec((tm, ROUTER_LANES), lambda i: (i, 0)),
                   pl.BlockSpec((tm, ROUTER_LANES), lambda i: (i, 0))],
        out_shape=[jax.ShapeDtypeStruct((m, ROUTER_LANES), jnp.int32),
                   jax.ShapeDtypeStruct((m, ROUTER_LANES), F32)],
        compiler_params=_params("parallel"),
        name="router",
    )(x, gain, wh, wl, bias)


def _row_copy(src_hbm, src_row, dst, dst_row, sem):
    return pltpu.make_async_copy(src_hbm.at[pl.ds(src_row, 1)], dst.at[pl.ds(dst_row, 1)], sem)


def _gather_kernel(nu_ref, tok_ref, tokn_ref, x_hbm, g_ref, out_ref, buf, sems, *, rows):
    i = pl.program_id(0)
    n_live = nu_ref[0]
    slot = i % 2

    def fetch(idx_ref, sl):
        def start(r, carry):
            _row_copy(x_hbm, idx_ref[0, 0, r], buf.at[sl], r, sems.at[sl]).start()
            return carry

        lax.fori_loop(0, rows, start, 0, unroll=8)

    @pl.when(i == 0)
    def _():
        fetch(tok_ref, 0)

    @pl.when(i + 1 < n_live)
    def _():
        fetch(tokn_ref, 1 - slot)

    @pl.when(i < n_live)
    def _():
        def wait(r, carry):
            _row_copy(x_hbm, 0, buf.at[slot], r, sems.at[slot]).wait()
            return carry

        lax.fori_loop(0, rows, wait, 0, unroll=8)
        out_ref[...] = _rms(buf[slot], g_ref[...]).astype(out_ref.dtype)

    @pl.when(i >= n_live)
    def _():
        out_ref[...] = jnp.zeros_like(out_ref)


def _gather_rows(x, gain, row_tok, n_used, *, rows):
    n_rows = row_tok.shape[0]
    d = x.shape[1]
    nb = n_rows // rows
    tok3 = row_tok.reshape(nb, 1, rows)
    grid_spec = pltpu.PrefetchScalarGridSpec(
        num_scalar_prefetch=1,
        grid=(nb,),
        in_specs=[pl.BlockSpec((1, 1, rows), lambda i, nu: (i, 0, 0), memory_space=pltpu.SMEM),
                  pl.BlockSpec((1, 1, rows), lambda i, nu: (jnp.minimum(i + 1, nb - 1), 0, 0),
                               memory_space=pltpu.SMEM),
                  pl.BlockSpec(memory_space=pl.ANY),
                  pl.BlockSpec((1, d), lambda i, nu: (0, 0))],
        out_specs=pl.BlockSpec((rows, d), lambda i, nu: (i, 0)),
        scratch_shapes=[pltpu.VMEM((2, rows, d), F32), pltpu.SemaphoreType.DMA((2,))],
    )
    return pl.pallas_call(
        functools.partial(_gather_kernel, rows=rows),
        grid_spec=grid_spec,
        out_shape=jax.ShapeDtypeStruct((n_rows, d), BF16),
        compiler_params=_params("arbitrary"),
        name="moe_gather",
    )(n_used, tok3, tok3, x, gain)


def _moe_up_kernel(be_ref, nu_ref, xs_ref, wg_ref, wu_ref, h_ref, wgb_ref, wub_ref):
    i = pl.program_id(1)
    prev = be_ref[jnp.maximum(i - 1, 0)]

    @pl.when(jnp.logical_or(i == 0, be_ref[i] != prev))
    def _():
        wgb_ref[...] = wg_ref[...].astype(BF16)
        wub_ref[...] = wu_ref[...].astype(BF16)

    @pl.when(i < nu_ref[0])
    def _():
        xn = xs_ref[...]
        a = jnp.dot(xn, wgb_ref[...], preferred_element_type=F32)
        b = jnp.dot(xn, wub_ref[...], preferred_element_type=F32)
        h_ref[...] = (a * _sigmoid(a) * b).astype(h_ref.dtype)

    @pl.when(i >= nu_ref[0])
    def _():
        h_ref[...] = jnp.zeros_like(h_ref)


def _moe_up(xs, w_gate, w_up, layer, blk_exp, n_used, *, bm, tn=512):
    n_rows, d = xs.shape
    de = w_gate.shape[-1]
    tn = _tile(de, tn)
    nb = n_rows // bm

    def row_map(j, i, be, nu):
        return (jnp.minimum(i, nu[0] - 1), 0)

    def w_map(j, i, be, nu):
        return (layer, be[i], 0, j)

    grid_spec = pltpu.PrefetchScalarGridSpec(
        num_scalar_prefetch=2,
        grid=(de // tn, nb),
        in_specs=[pl.BlockSpec((bm, d), row_map),
                  pl.BlockSpec((None, None, d, tn), w_map),
                  pl.BlockSpec((None, None, d, tn), w_map)],
        out_specs=pl.BlockSpec((bm, tn), lambda j, i, be, nu: (i, j)),
        scratch_shapes=[pltpu.VMEM((d, tn), BF16), pltpu.VMEM((d, tn), BF16)],
    )
    return pl.pallas_call(
        _moe_up_kernel,
        grid_spec=grid_spec,
        out_shape=jax.ShapeDtypeStruct((n_rows, de), BF16),
        compiler_params=_params("arbitrary", "arbitrary"),
        name="moe_up",
    )(blk_exp, n_used, xs, w_gate, w_up)


def _moe_down_kernel(be_ref, nu_ref, h_ref, wd_ref, y_ref, wdb_ref):
    i = pl.program_id(1)
    prev = be_ref[jnp.maximum(i - 1, 0)]

    @pl.when(jnp.logical_or(i == 0, be_ref[i] != prev))
    def _():
        wdb_ref[...] = wd_ref[...].astype(BF16)

    @pl.when(i < nu_ref[0])
    def _():
        y_ref[...] = jnp.dot(h_ref[...], wdb_ref[...], preferred_element_type=F32)

    @pl.when(i >= nu_ref[0])
    def _():
        y_ref[...] = jnp.zeros_like(y_ref)


def _moe_down(hs, w_down, layer, blk_exp, n_used, *, bm, tn=2048):
    n_rows, de = hs.shape
    d = w_down.shape[-1]
    tn = _tile(d, tn)
    nb = n_rows // bm
    grid_spec = pltpu.PrefetchScalarGridSpec(
        num_scalar_prefetch=2,
        grid=(d // tn, nb),
        in_specs=[pl.BlockSpec((bm, de), lambda j, i, be, nu: (jnp.minimum(i, nu[0] - 1), 0)),
                  pl.BlockSpec((None, None, de, tn), lambda j, i, be, nu: (layer, be[i], 0, j))],
        out_specs=pl.BlockSpec((bm, tn), lambda j, i, be, nu: (i, j)),
        scratch_shapes=[pltpu.VMEM((de, tn), BF16)],
    )
    return pl.pallas_call(
        _moe_down_kernel,
        grid_spec=grid_spec,
        out_shape=jax.ShapeDtypeStruct((n_rows, d), F32),
        compiler_params=_params("arbitrary", "arbitrary"),
        name="moe_down",
    )(blk_exp, n_used, hs, w_down)


def _combine_kernel(pos_ref, posn_ref, x_ref, ew_ref, g_ref, ys_hbm, out_ref, ybuf, sems, *, tm, nblk, final):
    i = pl.program_id(0)
    slot = i % 2

    def fetch(idx_ref, sl):
        def start(r, carry):
            for k in range(TOP_K):
                _row_copy(ys_hbm, idx_ref[0, 0, TOP_K * r + k], ybuf.at[sl, k], r, sems.at[sl]).start()
            return carry

        lax.fori_loop(0, tm, start, 0, unroll=4)

    @pl.when(i == 0)
    def _():
        fetch(pos_ref, 0)

    @pl.when(i + 1 < nblk)
    def _():
        fetch(posn_ref, 1 - slot)

    def wait(r, carry):
        for k in range(TOP_K):
            _row_copy(ys_hbm, 0, ybuf.at[slot, k], r, sems.at[slot]).wait()
        return carry

    lax.fori_loop(0, tm, wait, 0, unroll=4)
    w = ew_ref[...]
    y = x_ref[...]
    for k in range(TOP_K):
        y = y + w[:, k:k + 1] * ybuf[slot, k]
    if final:
        y = _rms(y, g_ref[...])
    out_ref[...] = y


def _combine(x, ew, pos, ys, gain, *, final, blk0=0, nblk=None, tm=256):
    m, d = x.shape
    tm = _tile(m, tm, SUBLANES)
    nb = m // tm
    nblk = nb if nblk is None else nblk
    pos3 = pos.reshape(nb, 1, TOP_K * tm)
    return pl.pallas_call(
        functools.partial(_combine_kernel, tm=tm, nblk=nblk, final=final),
        grid=(nblk,),
        in_specs=[pl.BlockSpec((1, 1, TOP_K * tm), lambda i: (blk0 + i, 0, 0), memory_space=pltpu.SMEM),
                  pl.BlockSpec((1, 1, TOP_K * tm), lambda i: (jnp.minimum(blk0 + i + 1, nb - 1), 0, 0),
                               memory_space=pltpu.SMEM),
                  pl.BlockSpec((tm, d), lambda i: (blk0 + i, 0)),
                  pl.BlockSpec((tm, ROUTER_LANES), lambda i: (blk0 + i, 0)),
                  pl.BlockSpec((1, d), lambda i: (0, 0)),
                  pl.BlockSpec(memory_space=pl.ANY)],
        out_specs=pl.BlockSpec((tm, d), lambda i: (i, 0)),
        out_shape=jax.ShapeDtypeStruct((nblk * tm, d), F32),
        scratch_shapes=[pltpu.VMEM((2, TOP_K, tm, d), F32), pltpu.SemaphoreType.DMA((2,))],
        compiler_params=_params("arbitrary"),
        name="moe_combine",
    )(pos3, pos3, x, ew, gain, ys)


def _dispatch_plan(eid, bm):
    n_pairs = eid.shape[0] * TOP_K
    flat_e = eid.reshape(-1)
    order = jnp.argsort(flat_e, stable=True).astype(jnp.int32)
    se = flat_e[order]
    bounds = jnp.searchsorted(se, jnp.arange(N_EXPERTS + 1, dtype=jnp.int32), side="left").astype(jnp.int32)
    start, counts = bounds[:-1], bounds[1:] - bounds[:-1]
    pcounts = (counts + bm - 1) // bm * bm
    pend = jnp.cumsum(pcounts)
    pstart = pend - pcounts
    dest = (pstart[se] + jnp.arange(n_pairs, dtype=jnp.int32) - start[se]).astype(jnp.int32)
    _, pos = lax.sort((order, dest), num_keys=1)
    n_rows = n_pairs + N_EXPERTS * bm
    blk_exp = jnp.minimum(
        jnp.searchsorted(pend, jnp.arange(n_rows // bm, dtype=jnp.int32) * bm, side="right"), N_EXPERTS - 1
    ).astype(jnp.int32)
    blk_row0 = jnp.arange(n_rows // bm, dtype=jnp.int32) * bm
    base = start[blk_exp] + blk_row0 - pstart[blk_exp]
    n_real = jnp.clip(counts[blk_exp] - (blk_row0 - pstart[blk_exp]), 0, bm)
    order_pad = jnp.concatenate([order, jnp.zeros((bm,), jnp.int32)])
    blocks = jax.vmap(lambda s: lax.dynamic_slice(order_pad, (s,), (bm,)))(jnp.clip(base, 0, n_pairs))
    real = jnp.arange(bm, dtype=jnp.int32)[None, :] < n_real[:, None]
    row_tok = jnp.where(real, blocks // TOP_K, 0).reshape(n_rows).astype(jnp.int32)
    n_used = (pend[-1:] // bm).astype(jnp.int32)
    return row_tok, pos.astype(jnp.int32), blk_exp, n_used


def kernel(x_prompt, x_sample, mem_prompt, mem_sample, g_mix, w_in, a_lb_fwd, a_lb_bwd, a_norm, b_g_cq, b_w_uq, b_g_ckv, b_w_ukv, b_norm, c_conv_w, c_conv_b, c_norm, w_out, g_xattn, g_mem, w_xq, w_xk, w_xv, w_xo, g_ffn, w_rg, b_rg, w_re, b_re, w_gate, w_up, w_down, g_final):
    depth, d, _ = w_in.shape
    seq = x_prompt.shape[1]
    assert x_sample.shape[1] == seq and seq % CHUNK == 0
    n_mem = mem_prompt.shape[1]
    nb_p, nb_s = x_prompt.shape[0], x_sample.shape[0]
    nseq = nb_p + nb_s
    d_ak = a_lb_fwd.shape[1]
    a_heads = d_ak // LANES
    d_a = a_heads * A_HEAD_DIM
    q_lora = b_w_uq.shape[1]
    kv_lora = b_w_ukv.shape[1]
    b_heads = b_w_uq.shape[2] // (B_NOPE + B_ROPE)
    d_b = b_heads * B_V
    d_c = c_conv_b.shape[1]
    xw = w_xq.shape[2]
    assert d_a + d_b + d_c == d

    x = jnp.concatenate([x_prompt.reshape(-1, d), x_sample.reshape(-1, d)], axis=0)
    mem = jnp.concatenate([mem_prompt.reshape(-1, d), mem_sample.reshape(-1, d)], axis=0)
    n_tok = x.shape[0]

    sizes = (d_ak, d_ak, d_ak, d_a, d_a, q_lora, kv_lora, B_ROPE, d_c, d_c, d_c)
    offs = [0]
    for s in sizes:
        offs.append(offs[-1] + s)
    kr_pad = LANES - B_ROPE
    zw_used = offs[-1] + kr_pad
    zw = _round_up(zw_used, MXU_DIM)
    w_in_z = jnp.concatenate(
        [w_in[:, :, :offs[8]], jnp.zeros((depth, d, kr_pad), w_in.dtype), w_in[:, :, offs[8]:],
         jnp.zeros((depth, d, zw - zw_used), w_in.dtype)], axis=2).astype(BF16)
    zoff = {"a_q": offs[0], "a_ff": offs[1], "a_fb": offs[2], "a_i": offs[3], "a_g": offs[4], "b_cq": offs[5],
            "b_ckv": offs[6], "b_kr": offs[7], "c_b": offs[8] + kr_pad, "c_c": offs[9] + kr_pad,
            "c_h": offs[10] + kr_pad}
    assert all(v % LANES == 0 for v in zoff.values())
    assert zoff["b_cq"] % q_lora == 0 and zoff["b_ckv"] % kv_lora == 0

    qk_pad = 2 * LANES - (B_NOPE + B_ROPE)
    w_uq_z = jnp.pad(b_w_uq.reshape(depth, q_lora, b_heads, B_NOPE + B_ROPE),
                     ((0, 0), (0, 0), (0, 0), (0, qk_pad))).reshape(depth, q_lora, b_heads * 2 * LANES).astype(BF16)
    w_ukv_z = b_w_ukv.reshape(depth, kv_lora, b_heads, 2, B_V).transpose(0, 1, 3, 2, 4).reshape(
        depth, kv_lora, 2 * d_b).astype(BF16)
    w_out_z = w_out.astype(BF16)
    w_xq_z = w_xq.astype(BF16)
    w_xkv_z = jnp.concatenate([w_xk, w_xv], axis=2).astype(BF16)
    w_xo_z = w_xo.astype(BF16)
    w_r = jnp.concatenate([w_rg, w_re, jnp.zeros((depth, d, ROUTER_LANES - N_GROUPS - N_EXPERTS), F32)], axis=2)
    w_r_hi = w_r.astype(BF16)
    w_r_lo = (w_r - w_r_hi.astype(F32)).astype(BF16)
    b_r = jnp.concatenate([b_rg, b_re, jnp.zeros((depth, ROUTER_LANES - N_GROUPS - N_EXPERTS), F32)], axis=1)

    def lower_bounds(p):
        lb = jnp.cumsum(jax.nn.softmax(p.astype(F32), axis=0), axis=0)
        lb = lb - lb[0]
        return jnp.stack([jnp.log(lb), jnp.log1p(-lb), 1.0 - lb], axis=1)

    lbp = jnp.concatenate([lower_bounds(a_lb_fwd), lower_bounds(a_lb_bwd)], axis=1)

    inv = 1.0 / (ROPE_THETA ** (jnp.arange(0, B_ROPE, 2, dtype=F32) / B_ROPE))
    ang = jnp.arange(seq, dtype=F32)[:, None] * inv[None, :]
    zpad = jnp.zeros((seq, LANES - B_ROPE), F32)
    rope_c = jnp.concatenate([jnp.cos(ang), jnp.cos(ang), zpad], axis=1)
    rope_s = jnp.concatenate([jnp.sin(ang), jnp.sin(ang), zpad], axis=1)

    bm = 256 if n_tok * TOP_K >= N_EXPERTS * 256 else 16

    for l in range(depth):
        z = _matmul([(x, d, 0, 0)], w_in_z[l], gain=g_mix[l], out_dtype=F32, tm=512, tn=768, name="in_proj")
        o_a = _hgrn(z, lbp[l], a_norm[l].reshape(1, d_a), nseq=nseq, seq=seq, heads=a_heads,
                    cb_q=zoff["a_q"] // LANES, cb_ff=zoff["a_ff"] // LANES, cb_fb=zoff["a_fb"] // LANES,
                    cb_v=zoff["a_i"] // LANES, cb_g=zoff["a_g"] // LANES)
        q_raw = _matmul([(z, q_lora, zoff["b_cq"] // q_lora, 0)], w_uq_z[l], gain=b_g_cq[l], out_dtype=BF16,
                        tm=1024, tn=1024, name="q_up")
        kv_raw = _matmul([(z, kv_lora, zoff["b_ckv"] // kv_lora, 0)], w_ukv_z[l], gain=b_g_ckv[l],
                         out_dtype=BF16, tm=1024, tn=1024, name="kv_up")
        o_b = _mla(q_raw, kv_raw, z, rope_c, rope_s, b_norm[l].reshape(1, d_b), nseq=nseq, seq=seq,
                   heads=b_heads, cb_kr=zoff["b_kr"] // LANES)
        o_c = _conv(z, c_conv_w[l], c_conv_b[l].reshape(1, d_c), c_norm[l].reshape(1, d_c), nseq=nseq, seq=seq,
                    d_c=d_c, cb_b=zoff["c_b"] // LANES, cb_c=zoff["c_c"] // LANES, cb_h=zoff["c_h"] // LANES)
        kq = d_a
        assert d_b % kq == 0 and d_c == kq
        mix = [(o_a, kq, 0, 0)] + [(o_b, kq, i, 1 + i) for i in range(d_b // kq)] + [(o_c, kq, 0, 1 + d_b // kq)]
        x = _matmul(mix, w_out_z[l], res=x, out_dtype=F32, tm=512, tn=1024, name="out_proj")

        qx = _matmul([(x, d, 0, 0)], w_xq_z[l], gain=g_xattn[l], out_dtype=BF16, tm=512, tn=1024, name="xq_proj")
        kvx = _matmul([(mem, d, 0, 0)], w_xkv_z[l], gain=g_mem[l], out_dtype=BF16, tm=512, tn=1024, name="xkv_proj")
        ox = _xattn(qx, kvx, nseq=nseq, seq=seq, n_mem=n_mem, xw=xw)
        x = _matmul([(ox, xw, 0, 0)], w_xo_z[l], res=x, out_dtype=F32, tm=512, tn=1024, name="xo_proj")

        gain_f = g_ffn[l].reshape(1, d).astype(F32)
        eid, ew = _router(x, gain_f, w_r_hi[l], w_r_lo[l], b_r[l].reshape(1, ROUTER_LANES))
        row_tok, pos, blk_exp, n_used = _dispatch_plan(eid[:, :TOP_K], bm)
        xs = _gather_rows(x, gain_f, row_tok, n_used, rows=bm)
        hs = _moe_up(xs, w_gate, w_up, l, blk_exp, n_used, bm=bm)
        ys = _moe_down(hs, w_down, l, blk_exp, n_used, bm=bm)
        gain_o = g_final.reshape(1, d).astype(F32)
        if l < depth - 1:
            x = _combine(x, ew, pos, ys, gain_o, final=False)

    tmc = _tile(n_tok, 256, SUBLANES)
    assert (nb_p * seq) % tmc == 0
    nblk_p = nb_p * seq // tmc
    y_p = _combine(x, ew, pos, ys, gain_o, final=True, blk0=0, nblk=nblk_p)
    y_s = _combine(x, ew, pos, ys, gain_o, final=True, blk0=nblk_p, nblk=n_tok // tmc - nblk_p)
    return (y_p.reshape(nb_p, seq, d), y_s.reshape(nb_s, seq, d))
```
